```python
import math
import jax
import jax.numpy as jnp
from jax import lax
import numpy as np

D_MODEL = 4096
BATCH = 4
SEQ = 2048
DEPTH = 4
DEC_BATCH = 128
DEC_SEQ = 4
PAST_LEN = 16384
PAGE_SIZE = 128

N_MIXERS = 3
N_A = (DEPTH + 2) // 3
N_B = (DEPTH + 1) // 3
N_C = DEPTH // 3
N_META = 16
RET_HEADS = 16
RET_DK = D_MODEL // RET_HEADS
RET_DV = D_MODEL // RET_HEADS
RET_CHUNK = 128
ROPE_BASE = 10000.0
D_RNN = ((4 * D_MODEL // 3 + 255) // 256) * 256
LRU_BLOCKS = 16
LRU_BLOCK = D_RNN // LRU_BLOCKS
CONV_W = 4
LRU_C = 8.0
POOL_WINDOWS = (2, 4, 8, 16)
POOL_GROUPS = 4
POOL_GROUP = D_MODEL // POOL_GROUPS
POOL_MAX = 16
D_FF = ((8 * D_MODEL + 3 * 256 - 1) // (3 * 256)) * 256
ALPHA = (2.0 * DEPTH) ** 0.25
BETA = (8.0 * DEPTH) ** -0.25
LN_EPS = 1e-5

kernel_name = 'hybrid_retention_rglru_pool_decoder'

F32 = jnp.float32


def layer_norm(x, g, b):
    xf = x.astype(F32)
    mu = xf.mean(-1, keepdims=True)
    var = jnp.square(xf - mu).mean(-1, keepdims=True)
    return ((xf - mu) * lax.rsqrt(var + LN_EPS) * g + b).astype(x.dtype)


def swiglu(x, w1, w3, w2):
    return (jax.nn.silu(x @ w1) * (x @ w3)) @ w2


def rope(x, pos):
    half = x.shape[-1] // 2
    inv = ROPE_BASE ** (-jnp.arange(half, dtype=F32) / half)
    ang = pos[:, None] * inv[None, :]
    cos = jnp.cos(ang)[None, :, None, :]
    sin = jnp.sin(ang)[None, :, None, :]
    x1, x2 = x[..., :half], x[..., half:]
    return jnp.concatenate([x1 * cos - x2 * sin, x1 * sin + x2 * cos], axis=-1)


def retention_chunk(q, k, v, s0, log_g):
    L = q.shape[1]
    idx = jnp.arange(L, dtype=F32)
    diff = idx[:, None] - idx[None, :]
    decay = jnp.where(diff >= 0, jnp.exp(log_g[:, None, None] * jnp.maximum(diff, 0.0)), 0.0)
    scores = jnp.einsum('bqhd,bkhd->bhqk', q, k) * decay[None]
    o_inner = jnp.einsum('bhqk,bkhe->bqhe', scores, v)
    q_decay = jnp.exp(log_g[None, :] * (idx[:, None] + 1.0))
    o_cross = jnp.einsum('bqhd,bhde->bqhe', q, s0) * q_decay[None, :, :, None]
    k_decay = jnp.exp(log_g[None, :] * (L - 1.0 - idx[:, None]))
    s_new = (jnp.exp(log_g * L)[None, :, None, None] * s0
             + jnp.einsum('bkhd,bkhe->bhde', k * k_decay[None, :, :, None], v))
    return o_inner + o_cross, s_new


def retention_mixer(x, s0, pos0, is_prompt, w_in, gn_g, w_out):
    B, L, _ = x.shape
    q, k, v, g = jnp.split(x @ w_in, 4, axis=-1)
    pos = pos0 + jnp.arange(L, dtype=F32)
    q = rope(q.astype(F32).reshape(B, L, RET_HEADS, RET_DK), pos)
    k = rope(k.astype(F32).reshape(B, L, RET_HEADS, RET_DK), pos) * (RET_DK ** -0.5)
    v = v.astype(F32).reshape(B, L, RET_HEADS, RET_DV)
    log_g = jnp.log1p(-jnp.exp2(-5.0 - jnp.arange(RET_HEADS, dtype=F32)))
    if s0 is None:
        s0 = jnp.zeros((B, RET_HEADS, RET_DK, RET_DV), F32)
    s0 = s0.astype(F32)
    if is_prompt:
        o_meta, s = retention_chunk(q[:, :N_META], k[:, :N_META], v[:, :N_META], s0, log_g)
        n_real = L - N_META
        n_chunks = n_real // RET_CHUNK

        def to_chunks(t):
            return t[:, N_META:].reshape(B, n_chunks, RET_CHUNK, RET_HEADS, t.shape[-1]).swapaxes(0, 1)

        def body(s_c, qkv):
            qc, kc, vc = qkv
            o_c, s_n = retention_chunk(qc, kc, vc, s_c, log_g)
            return s_n, o_c

        s, o_rest = lax.scan(body, s, (to_chunks(q), to_chunks(k), to_chunks(v)))
        o_rest = o_rest.swapaxes(0, 1).reshape(B, n_real, RET_HEADS, RET_DV)
        o = jnp.concatenate([o_meta, o_rest], axis=1)
    else:
        o, s = retention_chunk(q, k, v, s0, log_g)
    mu = o.mean(-1, keepdims=True)
    var = jnp.square(o - mu).mean(-1, keepdims=True)
    o = ((o - mu) * lax.rsqrt(var + LN_EPS)).reshape(B, L, RET_HEADS * RET_DV) * gn_g
    y = (jax.nn.silu(g.astype(F32)) * o).astype(x.dtype) @ w_out
    return y, s


def rglru_mixer(x, h0, conv0, w_in, conv_w, conv_b, w_a, b_a, w_x, b_x, lam, w_out):
    B, L, _ = x.shape
    gate_br, xb = jnp.split(x @ w_in, 2, axis=-1)
    if conv0 is None:
        conv0 = jnp.zeros((B, CONV_W - 1, D_RNN), xb.dtype)
    xp = jnp.concatenate([conv0.astype(xb.dtype), xb], axis=1)
    xc = conv_b + sum(xp[:, j:j + L] * conv_w[j] for j in range(CONV_W))
    new_conv = xp[:, -(CONV_W - 1):]
    xf = xc.astype(F32)
    xblk = xf.reshape(B, L, LRU_BLOCKS, LRU_BLOCK)
    r = jax.nn.sigmoid(jnp.einsum('blnc,ncd->blnd', xblk, w_a).reshape(B, L, D_RNN) + b_a)
    i = jax.nn.sigmoid(jnp.einsum('blnc,ncd->blnd', xblk, w_x).reshape(B, L, D_RNN) + b_x)
    log_a = -LRU_C * jax.nn.softplus(-lam.astype(F32)) * r
    a = jnp.exp(log_a)
    b = jnp.sqrt(-jnp.expm1(2.0 * log_a)) * (i * xf)
    if h0 is not None:
        b = b.at[:, 0].add(a[:, 0] * h0.astype(F32))

    def combine(left, right):
        return (left[0] * right[0], right[0] * left[1] + right[1])

    _, h = lax.associative_scan(combine, (a, b), axis=1)
    y = (jax.nn.gelu(gate_br.astype(F32)) * h).astype(x.dtype) @ w_out
    return y, h[:, -1], new_conv


def pool_mixer(x, buf, pos0, w_pool, scale):
    B, L, D = x.shape
    P = POOL_MAX - 1
    if buf is None:
        buf = jnp.zeros((B, P, D), F32)
    xp = jnp.concatenate([buf.astype(F32), x.astype(F32)], axis=1)
    cs = jnp.concatenate([jnp.zeros((B, 1, D), F32), jnp.cumsum(xp, axis=1)], axis=1)
    pos = pos0 + jnp.arange(L)
    end = cs[:, P + 1:]
    groups = []
    for gi, w in enumerate(POOL_WINDOWS):
        lo, hi = gi * POOL_GROUP, (gi + 1) * POOL_GROUP
        wsum = end[..., lo:hi] - cs[:, P + 1 - w:P + 1 - w + L, lo:hi]
        cnt = jnp.minimum(w, pos + 1).astype(F32)[None, :, None]
        groups.append(wsum / cnt)
    mean = jnp.concatenate(groups, axis=-1)
    u = (mean - xp[:, P:]).reshape(B, L, POOL_GROUPS, POOL_GROUP)
    y = jnp.einsum('blgc,gcd->blgd', u, w_pool).reshape(B, L, D) * scale
    return y.astype(x.dtype), xp[:, -P:]


def trunk(x, ret_s, lru_h, lru_c, pool_b, pos0, is_prompt, weights):
    (ret_w_in, ret_gn_g, ret_w_out, lru_w_in, lru_conv_w, lru_conv_b, lru_w_a, lru_b_a,
     lru_w_x, lru_b_x, lru_lambda, lru_w_out, pool_w, pool_scale, ln_mix_g, ln_mix_b,
     ln_ffn_g, ln_ffn_b, ffn_w1, ffn_w3, ffn_w2) = weights
    out_ret, out_h, out_c, out_pool = [], [], [], []
    for i in range(DEPTH):
        j = i // N_MIXERS
        kind = i % N_MIXERS
        if kind == 0:
            y, s = retention_mixer(x, None if ret_s is None else ret_s[j], pos0, is_prompt,
                                   ret_w_in[j], ret_gn_g[j], ret_w_out[j])
            out_ret.append(s)
        elif kind == 1:
            y, h, c = rglru_mixer(x, None if lru_h is None else lru_h[j],
                                  None if lru_c is None else lru_c[j],
                                  lru_w_in[j], lru_conv_w[j], lru_conv_b[j], lru_w_a[j], lru_b_a[j],
                                  lru_w_x[j], lru_b_x[j], lru_lambda[j], lru_w_out[j])
            out_h.append(h)
            out_c.append(c)
        else:
            y, pb = pool_mixer(x, None if pool_b is None else pool_b[j], pos0, pool_w[j], pool_scale[j])
            out_pool.append(pb)
        x = layer_norm(ALPHA * x + y, ln_mix_g[i], ln_mix_b[i])
        x = layer_norm(ALPHA * x + swiglu(x, ffn_w1[i], ffn_w3[i], ffn_w2[i]), ln_ffn_g[i], ln_ffn_b[i])
    return x, jnp.stack(out_ret), jnp.stack(out_h), jnp.stack(out_c), jnp.stack(out_pool)


def setup_inputs(seed: int = 0) -> dict:
    key = jax.random.key(seed)
    ks = jax.random.split(key, 32)
    nrm = jax.random.normal
    u = jax.random.uniform(ks[17], (N_B, D_RNN), F32, minval=0.9, maxval=0.999)
    a_init = u ** (1.0 / LRU_C)
    lru_lambda = jnp.log(a_init) - jnp.log1p(-a_init)
    return {
        'x_prompt': nrm(ks[0], (BATCH, SEQ, D_MODEL), F32),
        'x_sample': nrm(ks[1], (DEC_BATCH, DEC_SEQ, D_MODEL), F32),
        'state_ret': 0.1 * nrm(ks[2], (N_A, DEC_BATCH, RET_HEADS, RET_DK, RET_DV), F32),
        'state_lru_h': 0.5 * nrm(ks[3], (N_B, DEC_BATCH, D_RNN), F32),
        'state_lru_conv': nrm(ks[4], (N_B, DEC_BATCH, CONV_W - 1, D_RNN), F32),
        'state_pool': nrm(ks[5], (N_C, DEC_BATCH, POOL_MAX - 1, D_MODEL), F32),
        'meta_tokens': nrm(ks[6], (N_META, D_MODEL), F32),
        'ret_w_in': nrm(ks[7], (N_A, D_MODEL, 4 * D_MODEL), F32) * D_MODEL ** -0.5,
        'ret_gn_g': 1.0 + 0.02 * nrm(ks[8], (N_A, RET_HEADS * RET_DV), F32),
        'ret_w_out': nrm(ks[9], (N_A, RET_HEADS * RET_DV, D_MODEL), F32) * (RET_HEADS * RET_DV) ** -0.5 * BETA,
        'lru_w_in': nrm(ks[10], (N_B, D_MODEL, 2 * D_RNN), F32) * D_MODEL ** -0.5,
        'lru_conv_w': nrm(ks[11], (N_B, CONV_W, D_RNN), F32) * CONV_W ** -0.5,
        'lru_conv_b': 0.01 * nrm(ks[12], (N_B, D_RNN), F32),
        'lru_w_a': nrm(ks[13], (N_B, LRU_BLOCKS, LRU_BLOCK, LRU_BLOCK), F32) * LRU_BLOCK ** -0.5,
        'lru_b_a': 0.01 * nrm(ks[14], (N_B, D_RNN), F32),
        'lru_w_x': nrm(ks[15], (N_B, LRU_BLOCKS, LRU_BLOCK, LRU_BLOCK), F32) * LRU_BLOCK ** -0.5,
        'lru_b_x': 0.01 * nrm(ks[16], (N_B, D_RNN), F32),
        'lru_lambda': lru_lambda,
        'lru_w_out': nrm(ks[18], (N_B, D_RNN, D_MODEL), F32) * D_RNN ** -0.5 * BETA,
        'pool_w': nrm(ks[19], (N_C, POOL_GROUPS, POOL_GROUP, POOL_GROUP), F32) * POOL_GROUP ** -0.5 * BETA,
        'pool_scale': 1.0 + 0.02 * nrm(ks[20], (N_C, D_MODEL), F32),
        'ln_mix_g': 1.0 + 0.02 * nrm(ks[21], (DEPTH, D_MODEL), F32),
        'ln_mix_b': 0.01 * nrm(ks[22], (DEPTH, D_MODEL), F32),
        'ln_ffn_g': 1.0 + 0.02 * nrm(ks[23], (DEPTH, D_MODEL), F32),
        'ln_ffn_b': 0.01 * nrm(ks[24], (DEPTH, D_MODEL), F32),
        'ffn_w1': nrm(ks[25], (DEPTH, D_MODEL, D_FF), F32) * D_MODEL ** -0.5,
        'ffn_w3': nrm(ks[26], (DEPTH, D_MODEL, D_FF), F32) * D_MODEL ** -0.5,
        'ffn_w2': nrm(ks[27], (DEPTH, D_FF, D_MODEL), F32) * D_FF ** -0.5 * BETA,
    }


def reference(x_prompt, x_sample, state_ret, state_lru_h, state_lru_conv, state_pool, meta_tokens,
              ret_w_in, ret_gn_g, ret_w_out, lru_w_in, lru_conv_w, lru_conv_b, lru_w_a, lru_b_a,
              lru_w_x, lru_b_x, lru_lambda, lru_w_out, pool_w, pool_scale, ln_mix_g, ln_mix_b,
              ln_ffn_g, ln_ffn_b, ffn_w1, ffn_w3, ffn_w2):
    weights = (ret_w_in, ret_gn_g, ret_w_out, lru_w_in, lru_conv_w, lru_conv_b, lru_w_a, lru_b_a,
               lru_w_x, lru_b_x, lru_lambda, lru_w_out, pool_w, pool_scale, ln_mix_g, ln_mix_b,
               ln_ffn_g, ln_ffn_b, ffn_w1, ffn_w3, ffn_w2)
    meta = jnp.broadcast_to(meta_tokens[None].astype(x_prompt.dtype), (x_prompt.shape[0], N_META, D_MODEL))
    xp = jnp.concatenate([meta, x_prompt], axis=1)
    yp, ret_p, h_p, conv_p, pool_p = trunk(xp, None, None, None, None, 0, True, weights)
    ys, ret_s, h_s, conv_s, pool_s = trunk(x_sample, state_ret, state_lru_h, state_lru_conv, state_pool,
                                           PAST_LEN, False, weights)
    return (yp[:, N_META:], ys, ret_p, ret_s, h_p, h_s, conv_p, conv_s, pool_p, pool_s)
```

```python
import functools
import math

import numpy as np
import jax
import jax.numpy as jnp
from jax import lax
from jax.experimental import pallas as pl
from jax.experimental.pallas import tpu as pltpu

F32 = jnp.float32
MXU_DTYPE = jnp.bfloat16

D_MODEL = 4096
N_META = 16
PAST_LEN = 16384
RET_HEADS = 16
RET_DK = D_MODEL // RET_HEADS
RET_CHUNK = 128
ROPE_BASE = 10000.0
D_RNN = 5632
LRU_BLOCKS = 16
LRU_BLOCK = D_RNN // LRU_BLOCKS
CONV_W = 4
LRU_C = 8.0
POOL_WINDOWS = (2, 4, 8, 16)
POOL_GROUP = D_MODEL // len(POOL_WINDOWS)
POOL_MAX = 16
D_FF = 11008
DEPTH = 4
ALPHA = (2.0 * DEPTH) ** 0.25
LN_EPS = 1e-5

V7X_SUBLANES = 8
V7X_LANES = 128
V7X_BF16_ROWS = 16
V7X_VMEM_LIMIT = 58 * 2 ** 20

ROW_TILE = 1104
ACC_ROW_TILE = 512
ACC_COL_CHUNK = 512
ACC_ROW_CHUNK = 64
D_FF_PAD = 11264
SEQ_TILE = 256
RET_HEAD_GROUP = 4
LRU_GROUP = 4 * LRU_BLOCK
POOL_SEQ_BLOCK = 32

_LOG_G = np.log1p(-np.exp2(-5.0 - np.arange(RET_HEADS, dtype=np.float64)))


def _cparams(n_axes):
    return pltpu.CompilerParams(dimension_semantics=("arbitrary",) * n_axes,
                                vmem_limit_bytes=V7X_VMEM_LIMIT)


def _mm(a, b):
    return jnp.dot(a.astype(MXU_DTYPE), b.astype(MXU_DTYPE), preferred_element_type=F32)


def _mm_nt(a, b):
    return lax.dot_general(a.astype(MXU_DTYPE), b.astype(MXU_DTYPE), (((1,), (1,)), ((), ())),
                           preferred_element_type=F32)


def _mm_tn(a, b):
    return lax.dot_general(a.astype(MXU_DTYPE), b.astype(MXU_DTYPE), (((0,), (0,)), ((), ())),
                           preferred_element_type=F32)


def _layer_norm_rows(y, g, b):
    mu = jnp.mean(y, axis=-1, keepdims=True)
    d = y - mu
    var = jnp.mean(d * d, axis=-1, keepdims=True)
    return d * lax.rsqrt(var + LN_EPS) * g + b


def _rope_cols(acc, cos, sin):
    half = RET_DK // 2
    parts = []
    for h in range(acc.shape[1] // RET_DK):
        x1 = acc[:, h * RET_DK: h * RET_DK + half]
        x2 = acc[:, h * RET_DK + half: (h + 1) * RET_DK]
        parts.append(x1 * cos - x2 * sin)
        parts.append(x1 * sin + x2 * cos)
    return jnp.concatenate(parts, axis=1)


def _proj_q_kernel(x_ref, w_ref, cos_ref, sin_ref, o_ref):
    acc = _mm(x_ref[...], w_ref[...])
    o_ref[...] = _rope_cols(acc, cos_ref[...], sin_ref[...]).astype(o_ref.dtype)


def _proj_k_kernel(x_ref, w_ref, cos_ref, sin_ref, e_ref, lg_ref, k_ref, kd_ref):
    acc = _mm(x_ref[...], w_ref[...])
    k = _rope_cols(acc, cos_ref[...], sin_ref[...]) * (RET_DK ** -0.5)
    k_ref[...] = k.astype(k_ref.dtype)
    kd_ref[...] = (k * jnp.exp(e_ref[...] * lg_ref[...])).astype(kd_ref.dtype)


def _proj_cast_kernel(x_ref, w_ref, o_ref):
    o_ref[...] = _mm(x_ref[...], w_ref[...]).astype(o_ref.dtype)


def _proj_silu_kernel(x_ref, w_ref, o_ref):
    acc = _mm(x_ref[...], w_ref[...])
    o_ref[...] = (acc * jax.nn.sigmoid(acc)).astype(o_ref.dtype)


def _proj_gelu_kernel(x_ref, w_ref, o_ref):
    o_ref[...] = jax.nn.gelu(_mm(x_ref[...], w_ref[...])).astype(o_ref.dtype)


def _proj_swiglu_kernel(x_ref, w1_ref, w3_ref, o_ref):
    x = x_ref[...]
    a = _mm(x, w1_ref[...])
    b = _mm(x, w3_ref[...])
    o_ref[...] = (a * jax.nn.sigmoid(a) * b).astype(o_ref.dtype)


def _project(body, x, weights, col_offsets, n_cols, bn, out_dtypes, row_inputs=(), col_inputs=(),
             name=None):
    rows, k = x.shape
    bm = min(ROW_TILE, -(-rows // V7X_BF16_ROWS) * V7X_BF16_ROWS)
    assert n_cols % bn == 0 and all(off % bn == 0 for off in col_offsets)
    grid = (pl.cdiv(rows, bm), n_cols // bn)
    in_specs = [pl.BlockSpec((bm, k), lambda i, j: (i, 0))]
    for off in col_offsets:
        in_specs.append(pl.BlockSpec((k, bn), functools.partial(lambda i, j, o: (0, j + o), o=off // bn)))
    for r in row_inputs:
        in_specs.append(pl.BlockSpec((bm, r.shape[1]), lambda i, j: (i, 0)))
    for c in col_inputs:
        in_specs.append(pl.BlockSpec((1, bn), lambda i, j: (0, j)))
    out_shape = [jax.ShapeDtypeStruct((rows, n_cols), dt) for dt in out_dtypes]
    out_specs = [pl.BlockSpec((bm, bn), lambda i, j: (i, j)) for _ in out_dtypes]
    outs = pl.pallas_call(
        body, grid=grid, in_specs=in_specs, out_specs=out_specs, out_shape=out_shape,
        compiler_params=_cparams(2), name=name,
    )(x, *weights, *row_inputs, *col_inputs)
    return outs


def _residual_ln_kernel(a_ref, w_ref, x_ref, g_ref, b_ref, o_ref, obf_ref):
    k = pl.program_id(1)
    bm, d = o_ref.shape

    @pl.when(k == 0)
    def _():
        o_ref[...] = ALPHA * x_ref[...]

    a = a_ref[...]
    for n in range(d // ACC_COL_CHUNK):
        cs = slice(n * ACC_COL_CHUNK, (n + 1) * ACC_COL_CHUNK)
        o_ref[:, cs] += _mm(a, w_ref[:, cs])

    @pl.when(k == pl.num_programs(1) - 1)
    def _():
        rc = ACC_ROW_CHUNK if bm % ACC_ROW_CHUNK == 0 else bm
        for r in range(bm // rc):
            rs = slice(r * rc, (r + 1) * rc)
            out = _layer_norm_rows(o_ref[rs, :], g_ref[...], b_ref[...])
            o_ref[rs, :] = out
            obf_ref[rs, :] = out.astype(obf_ref.dtype)


def _residual_ln(a, w, x, g, b, bk, name=None):
    rows, kdim = a.shape
    d = w.shape[1]
    bm = min(ACC_ROW_TILE, -(-rows // V7X_BF16_ROWS) * V7X_BF16_ROWS)
    assert kdim % bk == 0
    grid = (pl.cdiv(rows, bm), kdim // bk)
    return pl.pallas_call(
        _residual_ln_kernel, grid=grid,
        in_specs=[pl.BlockSpec((bm, bk), lambda i, k: (i, k)),
                  pl.BlockSpec((bk, d), lambda i, k: (k, 0)),
                  pl.BlockSpec((bm, d), lambda i, k: (i, 0)),
                  pl.BlockSpec((1, d), lambda i, k: (0, 0)),
                  pl.BlockSpec((1, d), lambda i, k: (0, 0))],
        out_specs=[pl.BlockSpec((bm, d), lambda i, k: (i, 0)),
                   pl.BlockSpec((bm, d), lambda i, k: (i, 0))],
        out_shape=[jax.ShapeDtypeStruct((rows, d), F32), jax.ShapeDtypeStruct((rows, d), MXU_DTYPE)],
        compiler_params=_cparams(2), name=name,
    )(a, w, x, g.reshape(1, d), b.reshape(1, d))


def _group_norm_gate(o, sg, gn):
    mu = jnp.mean(o, axis=-1, keepdims=True)
    d = o - mu
    var = jnp.mean(d * d, axis=-1, keepdims=True)
    return sg * (d * lax.rsqrt(var + LN_EPS) * gn)


def _ret_prompt_kernel(q_ref, k_ref, kd_ref, v_ref, sg_ref,
                       qm_ref, km_ref, kdm_ref, vm_ref, sgm_ref,
                       gn_ref, dec_ref, qd_ref, cd_ref, z_ref, zm_ref, st_ref):
    c = pl.program_id(2)
    heads = dec_ref.shape[0]

    @pl.when(c == 0)
    def _():
        for h in range(heads):
            cs = slice(h * RET_DK, (h + 1) * RET_DK)
            q, k, kd, v = qm_ref[:, cs], km_ref[:, cs], kdm_ref[:, cs], vm_ref[:, cs]
            scores = _mm_nt(q, k) * dec_ref[h, :N_META, :N_META]
            o = _mm(scores, v)
            zm_ref[:, cs] = _group_norm_gate(o, sgm_ref[:, cs], gn_ref[:, cs]).astype(zm_ref.dtype)
            st_ref[0, h] = _mm_tn(kd, v)

    for h in range(heads):
        cs = slice(h * RET_DK, (h + 1) * RET_DK)
        q, k, kd, v = q_ref[:, cs], k_ref[:, cs], kd_ref[:, cs], v_ref[:, cs]
        s = st_ref[0, h]
        scores = _mm_nt(q, k) * dec_ref[h]
        o = _mm(scores, v) + _mm(q, s) * qd_ref[h]
        z_ref[:, cs] = _group_norm_gate(o, sg_ref[:, cs], gn_ref[:, cs]).astype(z_ref.dtype)
        st_ref[0, h] = cd_ref[h] * s + _mm_tn(kd, v)


def _retention_tables(length):
    idx = np.arange(length, dtype=np.float64)
    diff = idx[:, None] - idx[None, :]
    dec = np.where(diff >= 0, np.exp(_LOG_G[:, None, None] * np.maximum(diff, 0.0)), 0.0)
    qd = np.exp(_LOG_G[:, None] * (idx[None, :] + 1.0))
    return dec.astype(np.float32), qd.astype(np.float32)


def _retention_prompt(q, k, kd, v, sg, gn, batch, seq, off_meta):
    rows = q.shape[0]
    nc = seq // RET_CHUNK
    hg = RET_HEAD_GROUP
    wcols = hg * RET_DK
    dec, qd = _retention_tables(RET_CHUNK)
    qd_full = np.broadcast_to(qd[:, :, None], (RET_HEADS, RET_CHUNK, RET_DK)).copy()
    cd = np.broadcast_to(np.exp(_LOG_G * RET_CHUNK)[:, None, None], (RET_HEADS, 1, RET_DK)).astype(np.float32)
    assert off_meta % N_META == 0 and seq % RET_CHUNK == 0
    real = pl.BlockSpec((RET_CHUNK, wcols), lambda b, g, c: (b * nc + c, g))
    meta = pl.BlockSpec((N_META, wcols), lambda b, g, c: (off_meta // N_META + b, g))

    return pl.pallas_call(
        _ret_prompt_kernel, grid=(batch, RET_HEADS // hg, nc),
        in_specs=[real] * 5 + [meta] * 5 + [
            pl.BlockSpec((1, wcols), lambda b, g, c: (0, g)),
            pl.BlockSpec((hg, RET_CHUNK, RET_CHUNK), lambda b, g, c: (g, 0, 0)),
            pl.BlockSpec((hg, RET_CHUNK, RET_DK), lambda b, g, c: (g, 0, 0)),
            pl.BlockSpec((hg, 1, RET_DK), lambda b, g, c: (g, 0, 0))],
        out_specs=[real,
                   pl.BlockSpec((N_META, wcols), lambda b, g, c: (b, g)),
                   pl.BlockSpec((1, hg, RET_DK, RET_DK), lambda b, g, c: (b, g, 0, 0))],
        out_shape=[jax.ShapeDtypeStruct((rows, D_MODEL), MXU_DTYPE),
                   jax.ShapeDtypeStruct((batch * N_META, D_MODEL), MXU_DTYPE),
                   jax.ShapeDtypeStruct((batch, RET_HEADS, RET_DK, RET_DK), F32)],
        compiler_params=_cparams(3), name="retention_prompt",
    )(q, k, kd, v, sg, q, k, kd, v, sg, gn.reshape(1, D_MODEL), jnp.asarray(dec), jnp.asarray(qd_full),
      jnp.asarray(cd))


def _ret_sample_kernel(q_ref, k_ref, kd_ref, v_ref, sg_ref, s0_ref, gn_ref, dec_ref, qd_ref, cd_ref,
                       z_ref, st_ref):
    for h in range(RET_HEADS):
        cs = slice(h * RET_DK, (h + 1) * RET_DK)
        q, k, kd, v = q_ref[:, cs], k_ref[:, cs], kd_ref[:, cs], v_ref[:, cs]
        s = s0_ref[h]
        scores = _mm_nt(q, k) * dec_ref[h]
        o = _mm(scores, v) + _mm(q, s) * qd_ref[h]
        z_ref[:, cs] = _group_norm_gate(o, sg_ref[:, cs], gn_ref[:, cs]).astype(z_ref.dtype)
        st_ref[h] = cd_ref[h] * s + _mm_tn(kd, v)


def _retention_sample(q, k, kd, v, sg, gn, state, layer, dec_seq):
    db, pad_rows, _ = q.shape
    dec, qd = _retention_tables(dec_seq)
    dec_p = np.zeros((RET_HEADS, pad_rows, pad_rows), np.float32)
    dec_p[:, :dec_seq, :dec_seq] = dec
    qd_p = np.zeros((RET_HEADS, pad_rows, RET_DK), np.float32)
    qd_p[:, :dec_seq, :] = qd[:, :, None]
    cd = np.broadcast_to(np.exp(_LOG_G * dec_seq)[:, None, None], (RET_HEADS, 1, RET_DK)).astype(np.float32)
    tok = pl.BlockSpec((None, pad_rows, D_MODEL), lambda b: (b, 0, 0))
    return pl.pallas_call(
        _ret_sample_kernel, grid=(db,),
        in_specs=[tok] * 5 + [
            pl.BlockSpec((None, None, RET_HEADS, RET_DK, RET_DK), lambda b: (layer, b, 0, 0, 0)),
            pl.BlockSpec((1, D_MODEL), lambda b: (0, 0)),
            pl.BlockSpec((RET_HEADS, pad_rows, pad_rows), lambda b: (0, 0, 0)),
            pl.BlockSpec((RET_HEADS, pad_rows, RET_DK), lambda b: (0, 0, 0)),
            pl.BlockSpec((RET_HEADS, 1, RET_DK), lambda b: (0, 0, 0))],
        out_specs=[tok, pl.BlockSpec((None, RET_HEADS, RET_DK, RET_DK), lambda b: (b, 0, 0, 0))],
        out_shape=[jax.ShapeDtypeStruct((db, pad_rows, D_MODEL), MXU_DTYPE),
                   jax.ShapeDtypeStruct((db, RET_HEADS, RET_DK, RET_DK), F32)],
        compiler_params=_cparams(1), name="retention_sample",
    )(q, k, kd, v, sg, state, gn.reshape(1, D_MODEL), jnp.asarray(dec_p), jnp.asarray(qd_p), jnp.asarray(cd))


def _shift_rows(x, prev8, j):
    xr = pltpu.roll(x, j, 0)
    pr = pltpu.roll(prev8, j, 0)
    row = lax.broadcasted_iota(jnp.int32, prev8.shape, 0)
    first = jnp.where(row < j, pr, xr[:V7X_SUBLANES])
    if x.shape[0] == V7X_SUBLANES:
        return first
    return jnp.concatenate([first, xr[V7X_SUBLANES:]], axis=0)


def _lru_gates(xc, wa, wx, ba, bx, cneg):
    r = jax.nn.sigmoid(_mm(xc, wa) + ba)
    i = jax.nn.sigmoid(_mm(xc, wx) + bx)
    log_a = cneg * r
    a = jnp.exp(log_a)
    return a, jnp.sqrt(1.0 - a * a) * (i * xc)


def _lru_tile(x, gate, prev8, hprev, cw, cb, wa, wx, ba, bx, cneg, hbuf_ref):
    n = x.shape[0]
    xc = cb + cw[3:4] * x
    for j in range(1, CONV_W):
        xc = xc + cw[3 - j:4 - j] * _shift_rows(x, prev8, j)
    a, bb = _lru_gates(xc, wa, wx, ba, bx, cneg)
    row = lax.broadcasted_iota(jnp.int32, x.shape, 0) & (V7X_SUBLANES - 1)
    for s in (1, 2, 4):
        m = row >= s
        a_s = jnp.where(m, pltpu.roll(a, s, 0), 1.0)
        b_s = jnp.where(m, pltpu.roll(bb, s, 0), 0.0)
        bb = bb + a * b_s
        a = a * a_s
    h = hprev
    for g in range(n // V7X_SUBLANES):
        rs = slice(g * V7X_SUBLANES, (g + 1) * V7X_SUBLANES)
        hg = bb[rs] + a[rs] * h
        hbuf_ref[rs, :] = hg
        h = hg[V7X_SUBLANES - 1:V7X_SUBLANES]
    y = gate * hbuf_ref[0:n, :]
    return y, x[n - V7X_SUBLANES:], h


def _lru_prompt_kernel(x_ref, gate_ref, xm_ref, gatem_ref, cw_ref, cb_ref, wa_ref, wx_ref, ba_ref, bx_ref,
                       cneg_ref, y_ref, ym_ref, hl_ref, prev_ref, h_ref, hbuf_ref):
    c = pl.program_id(2)
    params = (cw_ref[...], cb_ref[...], wa_ref[...], wx_ref[...], ba_ref[...], bx_ref[...], cneg_ref[...])

    @pl.when(c == 0)
    def _():
        zeros8 = jnp.zeros(prev_ref.shape, F32)
        y, prev8, h = _lru_tile(xm_ref[...], gatem_ref[...], zeros8, zeros8[0:1], *params, hbuf_ref)
        ym_ref[...] = y.astype(ym_ref.dtype)
        prev_ref[...] = prev8
        h_ref[...] = jnp.broadcast_to(h, h_ref.shape)

    y, prev8, h = _lru_tile(x_ref[...], gate_ref[...], prev_ref[...], h_ref[0:1, :], *params, hbuf_ref)
    y_ref[...] = y.astype(y_ref.dtype)
    prev_ref[...] = prev8
    h_ref[...] = jnp.broadcast_to(h, h_ref.shape)
    hl_ref[...] = h


def _lru_param_specs(imap):
    cg = LRU_GROUP
    return [pl.BlockSpec((CONV_W, cg), lambda *a: (0, imap(*a))),
            pl.BlockSpec((1, cg), lambda *a: (0, imap(*a))),
            pl.BlockSpec((None, cg, cg), lambda *a: (imap(*a), 0, 0)),
            pl.BlockSpec((None, cg, cg), lambda *a: (imap(*a), 0, 0)),
            pl.BlockSpec((1, cg), lambda *a: (0, imap(*a))),
            pl.BlockSpec((1, cg), lambda *a: (0, imap(*a))),
            pl.BlockSpec((1, cg), lambda *a: (0, imap(*a)))]


def _lru_prompt(xb, gate, params, batch, seq, off_meta):
    rows = xb.shape[0]
    cg = LRU_GROUP
    nt = seq // SEQ_TILE
    assert seq % SEQ_TILE == 0 and off_meta % N_META == 0
    real = pl.BlockSpec((SEQ_TILE, cg), lambda g, b, c: (b * nt + c, g))
    meta = pl.BlockSpec((N_META, cg), lambda g, b, c: (off_meta // N_META + b, g))
    return pl.pallas_call(
        _lru_prompt_kernel, grid=(D_RNN // cg, batch, nt),
        in_specs=[real, real, meta, meta] + _lru_param_specs(lambda g, b, c: g),
        out_specs=[real,
                   pl.BlockSpec((N_META, cg), lambda g, b, c: (b, g)),
                   pl.BlockSpec((None, 1, cg), lambda g, b, c: (b, 0, g))],
        out_shape=[jax.ShapeDtypeStruct((rows, D_RNN), MXU_DTYPE),
                   jax.ShapeDtypeStruct((batch * N_META, D_RNN), MXU_DTYPE),
                   jax.ShapeDtypeStruct((batch, 1, D_RNN), F32)],
        scratch_shapes=[pltpu.VMEM((V7X_SUBLANES, cg), F32), pltpu.VMEM((V7X_SUBLANES, cg), F32),
                        pltpu.VMEM((SEQ_TILE, cg), F32)],
        compiler_params=_cparams(3), name="rglru_prompt",
    )(xb, gate, xb, gate, *params)


def _lru_sample_kernel(x_ref, gate_ref, c0_ref, h0_ref, cw_ref, cb_ref, wa_ref, wx_ref, ba_ref, bx_ref,
                       cneg_ref, y_ref, hl_ref):
    ds = x_ref.shape[0]
    cw, cb = cw_ref[...], cb_ref[...]
    xp = [c0_ref[j] for j in range(CONV_W - 1)] + [x_ref[t] for t in range(ds)]
    xcs = []
    for t in range(ds):
        xc = cb + cw[0:1] * xp[t]
        for j in range(1, CONV_W):
            xc = xc + cw[j:j + 1] * xp[t + j]
        xcs.append(xc)
    xc = jnp.concatenate(xcs, axis=0)
    a, bb = _lru_gates(xc, wa_ref[...], wx_ref[...], ba_ref[...], bx_ref[...], cneg_ref[...])
    db = x_ref.shape[1]
    h = h0_ref[...]
    for t in range(ds):
        rs = slice(t * db, (t + 1) * db)
        h = a[rs] * h + bb[rs]
        y_ref[t] = (gate_ref[t] * h).astype(y_ref.dtype)
    hl_ref[...] = h


def _lru_sample(xb, gate, conv0, h0, params):
    ds, db, _ = xb.shape
    cg = LRU_GROUP
    tok = pl.BlockSpec((ds, db, cg), lambda g: (0, 0, g))
    return pl.pallas_call(
        _lru_sample_kernel, grid=(D_RNN // cg,),
        in_specs=[tok, tok,
                  pl.BlockSpec((CONV_W - 1, db, cg), lambda g: (0, 0, g)),
                  pl.BlockSpec((db, cg), lambda g: (0, g))] + _lru_param_specs(lambda g: g),
        out_specs=[tok, pl.BlockSpec((db, cg), lambda g: (0, g))],
        out_shape=[jax.ShapeDtypeStruct((ds, db, D_RNN), MXU_DTYPE),
                   jax.ShapeDtypeStruct((db, D_RNN), F32)],
        compiler_params=_cparams(1), name="rglru_sample",
    )(xb, gate, conv0, h0, *params)


def _pool_project(means, x, w_ref, scale, g, b):
    ys = []
    for gi in range(len(POOL_WINDOWS)):
        cs = slice(gi * POOL_GROUP, (gi + 1) * POOL_GROUP)
        ys.append(_mm(means[gi] - x[:, cs], w_ref[gi]))
    y = jnp.concatenate(ys, axis=1) * scale
    return _layer_norm_rows(ALPHA * x + y, g, b)


def _pool_tile(x, carry, pos0, w_ref, scale, g, b):
    n = x.shape[0]
    pg = POOL_GROUP

    def shift(v, prev8, k):
        if k < V7X_SUBLANES:
            return _shift_rows(v, prev8, k)
        if n == V7X_SUBLANES:
            return prev8
        return jnp.concatenate([prev8, v[:n - V7X_SUBLANES]], axis=0)

    levels = [x]
    for li, k in enumerate((1, 2, 4, 8)):
        v = levels[-1][:, pg:] if li > 0 else levels[-1]
        levels.append(v + shift(v, carry[li][:, D_MODEL - v.shape[1]:], k))
    pos = (pos0 + lax.broadcasted_iota(jnp.int32, (n, 1), 0)).astype(F32)
    means = []
    for gi, w in enumerate(POOL_WINDOWS):
        cnt = jnp.minimum(float(w), pos + 1.0)
        means.append(levels[gi + 1][:, :pg] / cnt)
    out = _pool_project(means, x, w_ref, scale, g, b)
    new_carry = []
    for li in range(4):
        v = levels[li]
        tail = v[n - V7X_SUBLANES:]
        if v.shape[1] < D_MODEL:
            tail = jnp.concatenate([jnp.zeros((V7X_SUBLANES, D_MODEL - v.shape[1]), F32), tail], axis=1)
        new_carry.append(tail)
    return out, new_carry


def _pool_prompt_kernel(x_ref, xm_ref, w_ref, sc_ref, g_ref, b_ref, o_ref, obf_ref, om_ref, ombf_ref,
                        carry_ref):
    c = pl.program_id(1)
    consts = (w_ref, sc_ref[...], g_ref[...], b_ref[...])

    @pl.when(c == 0)
    def _():
        zero = jnp.zeros((V7X_SUBLANES, D_MODEL), F32)
        out, carry = _pool_tile(xm_ref[...], [zero] * 4, 0, *consts)
        om_ref[...] = out
        ombf_ref[...] = out.astype(ombf_ref.dtype)
        for li in range(4):
            carry_ref[li] = carry[li]

    out, carry = _pool_tile(x_ref[...], [carry_ref[li] for li in range(4)],
                            N_META + c * x_ref.shape[0], *consts)
    o_ref[...] = out
    obf_ref[...] = out.astype(obf_ref.dtype)
    for li in range(4):
        carry_ref[li] = carry[li]


def _pool_prompt(x, w, scale, g, b, batch, seq, off_meta):
    rows = x.shape[0]
    nt = seq // SEQ_TILE
    ng = len(POOL_WINDOWS)
    assert seq % SEQ_TILE == 0 and off_meta % N_META == 0
    real = pl.BlockSpec((SEQ_TILE, D_MODEL), lambda bb, c: (bb * nt + c, 0))
    vec = pl.BlockSpec((1, D_MODEL), lambda bb, c: (0, 0))
    meta_out = pl.BlockSpec((N_META, D_MODEL), lambda bb, c: (bb, 0))
    return pl.pallas_call(
        _pool_prompt_kernel, grid=(batch, nt),
        in_specs=[real, pl.BlockSpec((N_META, D_MODEL), lambda bb, c: (off_meta // N_META + bb, 0)),
                  pl.BlockSpec((ng, POOL_GROUP, POOL_GROUP), lambda bb, c: (0, 0, 0)), vec, vec, vec],
        out_specs=[real, real, meta_out, meta_out],
        out_shape=[jax.ShapeDtypeStruct((rows, D_MODEL), F32), jax.ShapeDtypeStruct((rows, D_MODEL), MXU_DTYPE),
                   jax.ShapeDtypeStruct((batch * N_META, D_MODEL), F32),
                   jax.ShapeDtypeStruct((batch * N_META, D_MODEL), MXU_DTYPE)],
        scratch_shapes=[pltpu.VMEM((4, V7X_SUBLANES, D_MODEL), F32)],
        compiler_params=_cparams(2), name="pool_prompt",
    )(x, x, w, scale.reshape(1, D_MODEL), g.reshape(1, D_MODEL), b.reshape(1, D_MODEL))


def _pool_sample_kernel(x_ref, buf_ref, w_ref, sc_ref, g_ref, b_ref, o_ref, obf_ref, *, pos0):
    ds = x_ref.shape[0]
    nbuf = buf_ref.shape[0]
    pg = POOL_GROUP
    means = []
    for gi, w in enumerate(POOL_WINDOWS):
        cs = slice(gi * pg, (gi + 1) * pg)
        xp = [buf_ref[j, :, cs] for j in range(nbuf - (w - 1), nbuf)] + [x_ref[t, :, cs] for t in range(ds)]
        rows = []
        for t in range(ds):
            acc = xp[t]
            for j in range(1, w):
                acc = acc + xp[t + j]
            rows.append(acc / float(min(w, pos0 + t + 1)))
        means.append(jnp.concatenate(rows, axis=0))
    x = jnp.concatenate([x_ref[t] for t in range(ds)], axis=0)
    out = _pool_project(means, x, w_ref, sc_ref[...], g_ref[...], b_ref[...])
    bs = x_ref.shape[1]
    for t in range(ds):
        o_ref[t] = out[t * bs:(t + 1) * bs]
        obf_ref[t] = out[t * bs:(t + 1) * bs].astype(obf_ref.dtype)


def _pool_sample(x, buf, w, scale, g, b, pos0):
    ds, db, _ = x.shape
    nbuf = buf.shape[0]
    bs = min(POOL_SEQ_BLOCK, db)
    ng = len(POOL_WINDOWS)
    assert db % bs == 0
    tok = pl.BlockSpec((ds, bs, D_MODEL), lambda s: (0, s, 0))
    vec = pl.BlockSpec((1, D_MODEL), lambda s: (0, 0))
    return pl.pallas_call(
        functools.partial(_pool_sample_kernel, pos0=pos0), grid=(db // bs,),
        in_specs=[tok, pl.BlockSpec((nbuf, bs, D_MODEL), lambda s: (0, s, 0)),
                  pl.BlockSpec((ng, POOL_GROUP, POOL_GROUP), lambda s: (0, 0, 0)), vec, vec, vec],
        out_specs=[tok, tok],
        out_shape=[jax.ShapeDtypeStruct((ds, db, D_MODEL), F32), jax.ShapeDtypeStruct((ds, db, D_MODEL), MXU_DTYPE)],
        compiler_params=_cparams(1), name="pool_sample",
    )(x, buf, w, scale.reshape(1, D_MODEL), g.reshape(1, D_MODEL), b.reshape(1, D_MODEL))


def _block_diag_groups(w):
    per = LRU_GROUP // LRU_BLOCK
    w = w.reshape(LRU_BLOCKS // per, per, LRU_BLOCK, LRU_BLOCK)
    eye = jnp.eye(per, dtype=w.dtype)
    dense = jnp.einsum('gpcd,pq->gpcqd', w, eye)
    return dense.reshape(LRU_BLOCKS // per, LRU_GROUP, LRU_GROUP)


def kernel(x_prompt, x_sample, state_ret, state_lru_h, state_lru_conv, state_pool, meta_tokens,
           ret_w_in, ret_gn_g, ret_w_out, lru_w_in, lru_conv_w, lru_conv_b, lru_w_a, lru_b_a,
           lru_w_x, lru_b_x, lru_lambda, lru_w_out, pool_w, pool_scale, ln_mix_g, ln_mix_b,
           ln_ffn_g, ln_ffn_b, ffn_w1, ffn_w3, ffn_w2):
    batch, seq, d = x_prompt.shape
    db, ds, _ = x_sample.shape
    assert d == D_MODEL
    n_real, n_samp, n_meta = batch * seq, db * ds, batch * N_META
    off_s, off_m = n_real, n_real + n_samp
    rows = off_m + n_meta
    mx = MXU_DTYPE

    x = jnp.concatenate([
        x_prompt.reshape(n_real, d),
        jnp.swapaxes(x_sample, 0, 1).reshape(n_samp, d),
        jnp.broadcast_to(meta_tokens[None].astype(x_prompt.dtype), (batch, N_META, d)).reshape(n_meta, d),
    ], axis=0)
    xbf = x.astype(mx)

    t_real = jnp.tile(jnp.arange(seq, dtype=F32), batch)
    t_samp = jnp.repeat(jnp.arange(ds, dtype=F32), db)
    t_meta = jnp.tile(jnp.arange(N_META, dtype=F32), batch)
    pos = jnp.concatenate([N_META + t_real, PAST_LEN + t_samp, t_meta])
    half = RET_DK // 2
    inv = ROPE_BASE ** (-jnp.arange(half, dtype=F32) / half)
    ang = pos[:, None] * inv[None, :]
    cos_t, sin_t = jnp.cos(ang), jnp.sin(ang)
    e_rows = jnp.concatenate([
        (RET_CHUNK - 1.0) - jnp.mod(t_real, float(RET_CHUNK)), (ds - 1.0) - t_samp, (N_META - 1.0) - t_meta,
    ])[:, None]
    lg_cols = jnp.repeat(jnp.asarray(_LOG_G, F32), RET_DK)[None, :]

    def sample_rows(a):
        return a[off_s:off_m].reshape(ds, db, a.shape[1])

    def merge(full, samp, meta):
        full = lax.dynamic_update_slice(full, samp.reshape(n_samp, samp.shape[-1]), (off_s, 0))
        return lax.dynamic_update_slice(full, meta, (off_m, 0))

    pad_rows = -(-ds // V7X_BF16_ROWS) * V7X_BF16_ROWS

    def to_seq_major(a):
        a = jnp.swapaxes(a, 0, 1)
        return jnp.pad(a, ((0, 0), (0, pad_rows - ds), (0, 0)))

    outs_ret_p, outs_ret_s, outs_h_p, outs_h_s = [], [], [], []
    outs_conv_p, outs_conv_s, outs_pool_p, outs_pool_s = [], [], [], []

    for i in range(DEPTH):
        j, kind = i // 3, i % 3
        if kind == 0:
            w_in = ret_w_in[j].astype(mx)
            (q,) = _project(_proj_q_kernel, xbf, [w_in], [0], D_MODEL, 512, [mx],
                            row_inputs=[cos_t, sin_t], name="ret_q")
            k, kd = _project(_proj_k_kernel, xbf, [w_in], [D_MODEL], D_MODEL, 512, [mx, mx],
                             row_inputs=[cos_t, sin_t, e_rows], col_inputs=[lg_cols], name="ret_k")
            (v,) = _project(_proj_cast_kernel, xbf, [w_in], [2 * D_MODEL], D_MODEL, 512, [mx], name="ret_v")
            (sg,) = _project(_proj_silu_kernel, xbf, [w_in], [3 * D_MODEL], D_MODEL, 512, [F32], name="ret_g")
            z, z_meta, s_p = _retention_prompt(q, k, kd, v, sg, ret_gn_g[j], batch, seq, off_m)
            z_s, s_s = _retention_sample(*(to_seq_major(sample_rows(a)) for a in (q, k, kd, v, sg)),
                                         ret_gn_g[j], state_ret, j, ds)
            z = merge(z, jnp.swapaxes(z_s[:, :ds], 0, 1), z_meta)
            x, xbf = _residual_ln(z, ret_w_out[j].astype(mx), x, ln_mix_g[i], ln_mix_b[i], 512, name="ret_out")
            outs_ret_p.append(s_p)
            outs_ret_s.append(s_s)
        elif kind == 1:
            w_in = lru_w_in[j].astype(mx)
            (gate,) = _project(_proj_gelu_kernel, xbf, [w_in], [0], D_RNN, 512, [F32], name="lru_gate")
            (xb,) = _project(_proj_cast_kernel, xbf, [w_in], [D_RNN], D_RNN, 512, [F32], name="lru_x")
            params = (lru_conv_w[j], lru_conv_b[j].reshape(1, D_RNN),
                      _block_diag_groups(lru_w_a[j]).astype(mx), _block_diag_groups(lru_w_x[j]).astype(mx),
                      lru_b_a[j].reshape(1, D_RNN), lru_b_x[j].reshape(1, D_RNN),
                      (-LRU_C * jax.nn.softplus(-lru_lambda[j].astype(F32))).reshape(1, D_RNN))
            y, y_meta, h_p = _lru_prompt(xb, gate, params, batch, seq, off_m)
            xb_s = sample_rows(xb)
            y_s, h_s = _lru_sample(xb_s, sample_rows(gate), jnp.swapaxes(state_lru_conv[j], 0, 1).astype(F32),
                                   state_lru_h[j].astype(F32), params)
            y = merge(y, y_s, y_meta)
            x, xbf = _residual_ln(y, lru_w_out[j].astype(mx), x, ln_mix_g[i], ln_mix_b[i], 512, name="lru_out")
            outs_h_p.append(h_p.reshape(batch, D_RNN))
            outs_h_s.append(h_s)
            assert seq >= CONV_W - 1
            outs_conv_p.append(xb[:n_real].reshape(batch, seq, D_RNN)[:, -(CONV_W - 1):])
            xp_s = jnp.concatenate([state_lru_conv[j].astype(F32), jnp.swapaxes(xb_s, 0, 1)], axis=1)
            outs_conv_s.append(xp_s[:, -(CONV_W - 1):])
        else:
            p = POOL_MAX - 1
            w_pool = pool_w[j].astype(mx)
            x_s = sample_rows(x)
            xo, xobf, xm, xmbf = _pool_prompt(x, w_pool, pool_scale[j], ln_mix_g[i], ln_mix_b[i], batch, seq, off_m)
            buf_tm = jnp.swapaxes(state_pool[j].astype(F32), 0, 1)
            xs, xsbf = _pool_sample(x_s, buf_tm, w_pool, pool_scale[j], ln_mix_g[i], ln_mix_b[i], PAST_LEN)
            x_in_p = x[:n_real].reshape(batch, seq, d)
            assert seq >= p
            outs_pool_p.append(x_in_p[:, -p:])
            xp_s = jnp.concatenate([state_pool[j].astype(F32), jnp.swapaxes(x_s, 0, 1)], axis=1)
            outs_pool_s.append(xp_s[:, -p:])
            x, xbf = merge(xo, xs, xm), merge(xobf, xsbf, xmbf)
        padc = ((0, 0), (0, D_FF_PAD - D_FF))
        w1 = jnp.pad(ffn_w1[i].astype(mx), padc)
        w3 = jnp.pad(ffn_w3[i].astype(mx), padc)
        w2 = jnp.pad(ffn_w2[i].astype(mx), ((0, D_FF_PAD - D_FF), (0, 0)))
        (hid,) = _project(_proj_swiglu_kernel, xbf, [w1, w3], [0, 0], D_FF_PAD, 512, [mx], name="ffn_up")
        x, xbf = _residual_ln(hid, w2, x, ln_ffn_g[i], ln_ffn_b[i], 512, name="ffn_down")

    y_prompt = x[:n_real].reshape(batch, seq, d)
    y_sample = jnp.swapaxes(x[off_s:off_m].reshape(ds, db, d), 0, 1)
    return (y_prompt, y_sample, jnp.stack(outs_ret_p), jnp.stack(outs_ret_s), jnp.stack(outs_h_p),
            jnp.stack(outs_h_s), jnp.stack(outs_conv_p), jnp.stack(outs_conv_s), jnp.stack(outs_pool_p),
            jnp.stack(outs_pool_s))
```

```python
import functools

import numpy as np
import jax
import jax.numpy as jnp
from jax import lax
from jax.experimental import pallas as pl
from jax.experimental.pallas import tpu as pltpu

F32 = jnp.float32
MXU_DTYPE = jnp.bfloat16

D_MODEL = 4096
N_META = 16
PAST_LEN = 16384
RET_HEADS = 16
RET_DK = D_MODEL // RET_HEADS
RET_CHUNK = 128
ROPE_BASE = 10000.0
D_RNN = 5632
LRU_BLOCKS = 16
LRU_BLOCK = D_RNN // LRU_BLOCKS
CONV_W = 4
LRU_C = 8.0
POOL_WINDOWS = (2, 4, 8, 16)
POOL_GROUP = D_MODEL // len(POOL_WINDOWS)
POOL_MAX = 16
D_FF = 11008
DEPTH = 4
ALPHA = (2.0 * DEPTH) ** 0.25
LN_EPS = 1e-5

V7X_SUBLANES = 8
V7X_LANES = 128
V7X_BF16_ROWS = 16
V7X_VMEM_LIMIT = 58 * 2 ** 20

ROW_TILE = 1104
ACC_ROW_TILE = 496
ACC_COL_CHUNK = 512
ACC_ROW_CHUNK = 16
D_FF_PAD = 11264
SEQ_TILE = 256
RET_HEAD_GROUP = 4
LRU_GROUP = 4 * LRU_BLOCK
POOL_SEQ_BLOCK = 32

_LOG_G = np.log1p(-np.exp2(-5.0 - np.arange(RET_HEADS, dtype=np.float64)))


def _cparams(n_axes):
    return pltpu.CompilerParams(dimension_semantics=("arbitrary",) * n_axes,
                                vmem_limit_bytes=V7X_VMEM_LIMIT)


def _mm(a, b):
    return jnp.dot(a.astype(MXU_DTYPE), b.astype(MXU_DTYPE), preferred_element_type=F32)


def _mm_nt(a, b):
    return lax.dot_general(a.astype(MXU_DTYPE), b.astype(MXU_DTYPE), (((1,), (1,)), ((), ())),
                           preferred_element_type=F32)


def _mm_tn(a, b):
    return lax.dot_general(a.astype(MXU_DTYPE), b.astype(MXU_DTYPE), (((0,), (0,)), ((), ())),
                           preferred_element_type=F32)


def _layer_norm_rows(y, g, b):
    mu = jnp.mean(y, axis=-1, keepdims=True)
    d = y - mu
    var = jnp.mean(d * d, axis=-1, keepdims=True)
    return d * lax.rsqrt(var + LN_EPS) * g + b


def _rope_cols(acc, cos, sin):
    half = RET_DK // 2
    parts = []
    for h in range(acc.shape[1] // RET_DK):
        x1 = acc[:, h * RET_DK: h * RET_DK + half]
        x2 = acc[:, h * RET_DK + half: (h + 1) * RET_DK]
        parts.append(x1 * cos - x2 * sin)
        parts.append(x1 * sin + x2 * cos)
    return jnp.concatenate(parts, axis=1)


def _proj_q_kernel(x_ref, w_ref, cos_ref, sin_ref, o_ref):
    acc = _mm(x_ref[...], w_ref[...])
    o_ref[...] = _rope_cols(acc, cos_ref[...], sin_ref[...]).astype(o_ref.dtype)


def _proj_k_kernel(x_ref, w_ref, cos_ref, sin_ref, e_ref, lg_ref, k_ref, kd_ref):
    acc = _mm(x_ref[...], w_ref[...])
    k = _rope_cols(acc, cos_ref[...], sin_ref[...]) * (RET_DK ** -0.5)
    k_ref[...] = k.astype(k_ref.dtype)
    kd_ref[...] = (k * jnp.exp(e_ref[...] * lg_ref[...])).astype(kd_ref.dtype)


def _proj_cast_kernel(x_ref, w_ref, o_ref):
    o_ref[...] = _mm(x_ref[...], w_ref[...]).astype(o_ref.dtype)


def _proj_silu_kernel(x_ref, w_ref, o_ref):
    acc = _mm(x_ref[...], w_ref[...])
    o_ref[...] = (acc * jax.nn.sigmoid(acc)).astype(o_ref.dtype)


def _proj_gelu_kernel(x_ref, w_ref, o_ref):
    o_ref[...] = jax.nn.gelu(_mm(x_ref[...], w_ref[...])).astype(o_ref.dtype)


def _proj_swiglu_kernel(x_ref, w1_ref, w3_ref, o_ref):
    x = x_ref[...]
    a = _mm(x, w1_ref[...])
    b = _mm(x, w3_ref[...])
    o_ref[...] = (a * jax.nn.sigmoid(a) * b).astype(o_ref.dtype)


def _project(body, x, weights, layer, col_offsets, n_cols, bn, out_dtypes, row_inputs=(), col_inputs=(),
             name=None):
    rows, k = x.shape
    bm = min(ROW_TILE, -(-rows // V7X_BF16_ROWS) * V7X_BF16_ROWS)
    assert n_cols % bn == 0 and all(off % bn == 0 for off in col_offsets)
    grid = (pl.cdiv(rows, bm), n_cols // bn)
    in_specs = [pl.BlockSpec((bm, k), lambda i, j: (i, 0))]
    for off in col_offsets:
        in_specs.append(pl.BlockSpec((None, k, bn),
                                     functools.partial(lambda i, j, o: (layer, 0, j + o), o=off // bn)))
    for r in row_inputs:
        in_specs.append(pl.BlockSpec((bm, r.shape[1]), lambda i, j: (i, 0)))
    for c in col_inputs:
        in_specs.append(pl.BlockSpec((1, bn), lambda i, j: (0, j)))
    out_shape = [jax.ShapeDtypeStruct((rows, n_cols), dt) for dt in out_dtypes]
    out_specs = [pl.BlockSpec((bm, bn), lambda i, j: (i, j)) for _ in out_dtypes]
    outs = pl.pallas_call(
        body, grid=grid, in_specs=in_specs, out_specs=out_specs, out_shape=out_shape,
        compiler_params=_cparams(2), name=name,
    )(x, *weights, *row_inputs, *col_inputs)
    return outs


def _residual_ln_kernel(a_ref, w_ref, x_hbm, g_ref, b_ref, o_ref, obf_ref, xbuf, sem, *, rows):
    i, k = pl.program_id(0), pl.program_id(1)
    n_tiles, n_k = pl.num_programs(0), pl.num_programs(1)
    bm, d = o_ref.shape
    tail = rows - (pl.cdiv(rows, bm) - 1) * bm

    def with_x_copy(fn):
        def copy(size):
            return pltpu.make_async_copy(
                x_hbm.at[pl.ds(pl.multiple_of(i * bm, V7X_BF16_ROWS), size), :],
                xbuf.at[pl.ds(0, size), :], sem)
        if tail == bm:
            fn(copy(bm))
        else:
            pl.when(i < n_tiles - 1)(lambda: fn(copy(bm)))
            pl.when(i == n_tiles - 1)(lambda: fn(copy(tail)))

    def accumulate(first):
        a = a_ref[...]
        for n in range(d // ACC_COL_CHUNK):
            cs = slice(n * ACC_COL_CHUNK, (n + 1) * ACC_COL_CHUNK)
            p = _mm(a, w_ref[:, cs])
            o_ref[:, cs] = p if first else o_ref[:, cs] + p

    @pl.when(k == 0)
    def _():
        with_x_copy(lambda cp: cp.start())
        accumulate(True)

    @pl.when(k > 0)
    def _():
        accumulate(False)

    @pl.when(k == n_k - 1)
    def _():
        with_x_copy(lambda cp: cp.wait())
        rc = ACC_ROW_CHUNK
        for r in range(bm // rc):
            rs = slice(r * rc, (r + 1) * rc)
            out = _layer_norm_rows(ALPHA * xbuf[rs, :] + o_ref[rs, :], g_ref[...], b_ref[...])
            o_ref[rs, :] = out
            obf_ref[rs, :] = out.astype(obf_ref.dtype)


def _residual_ln(a, w, layer, x, g, b, bk, name=None):
    rows, kdim = a.shape
    d = w.shape[2]
    bm = min(ACC_ROW_TILE, -(-rows // V7X_BF16_ROWS) * V7X_BF16_ROWS)
    assert kdim % bk == 0 and bm % ACC_ROW_CHUNK == 0
    n_tiles = pl.cdiv(rows, bm)
    assert n_tiles > 1 or rows == bm
    return pl.pallas_call(
        functools.partial(_residual_ln_kernel, rows=rows), grid=(n_tiles, kdim // bk),
        in_specs=[pl.BlockSpec((bm, bk), lambda i, k: (i, k)),
                  pl.BlockSpec((None, bk, d), lambda i, k: (layer, k, 0)),
                  pl.BlockSpec(memory_space=pl.ANY),
                  pl.BlockSpec((1, d), lambda i, k: (0, 0)),
                  pl.BlockSpec((1, d), lambda i, k: (0, 0))],
        out_specs=[pl.BlockSpec((bm, d), lambda i, k: (i, 0)),
                   pl.BlockSpec((bm, d), lambda i, k: (i, 0))],
        out_shape=[jax.ShapeDtypeStruct((rows, d), F32), jax.ShapeDtypeStruct((rows, d), MXU_DTYPE)],
        scratch_shapes=[pltpu.VMEM((bm, d), F32), pltpu.SemaphoreType.DMA],
        compiler_params=_cparams(2), name=name,
    )(a, w, x, g.reshape(1, d), b.reshape(1, d))


def _group_norm_gate(o, sg, gn):
    mu = jnp.mean(o, axis=-1, keepdims=True)
    d = o - mu
    var = jnp.mean(d * d, axis=-1, keepdims=True)
    return sg * (d * lax.rsqrt(var + LN_EPS) * gn)


def _ret_prompt_kernel(q_ref, k_ref, kd_ref, v_ref, sg_ref,
                       qm_ref, km_ref, kdm_ref, vm_ref, sgm_ref,
                       gn_ref, dec_ref, qd_ref, cd_ref, z_ref, zm_ref, st_ref):
    c = pl.program_id(2)
    heads = dec_ref.shape[0]

    @pl.when(c == 0)
    def _():
        for h in range(heads):
            cs = slice(h * RET_DK, (h + 1) * RET_DK)
            q, k, kd, v = qm_ref[:, cs], km_ref[:, cs], kdm_ref[:, cs], vm_ref[:, cs]
            scores = _mm_nt(q, k) * dec_ref[h, :N_META, :N_META]
            o = _mm(scores, v)
            zm_ref[:, cs] = _group_norm_gate(o, sgm_ref[:, cs], gn_ref[:, cs]).astype(zm_ref.dtype)
            st_ref[0, h] = _mm_tn(kd, v)

    for h in range(heads):
        cs = slice(h * RET_DK, (h + 1) * RET_DK)
        q, k, kd, v = q_ref[:, cs], k_ref[:, cs], kd_ref[:, cs], v_ref[:, cs]
        s = st_ref[0, h]
        scores = _mm_nt(q, k) * dec_ref[h]
        o = _mm(scores, v) + _mm(q, s) * qd_ref[h]
        z_ref[:, cs] = _group_norm_gate(o, sg_ref[:, cs], gn_ref[:, cs]).astype(z_ref.dtype)
        st_ref[0, h] = cd_ref[h] * s + _mm_tn(kd, v)


def _retention_tables(length):
    idx = np.arange(length, dtype=np.float64)
    diff = idx[:, None] - idx[None, :]
    dec = np.where(diff >= 0, np.exp(_LOG_G[:, None, None] * np.maximum(diff, 0.0)), 0.0)
    qd = np.exp(_LOG_G[:, None] * (idx[None, :] + 1.0))
    return dec.astype(np.float32), qd.astype(np.float32)


def _drop_ref(body, index):
    def wrapped(*refs):
        body(*refs[:index], *refs[index + 1:])
    return wrapped


def _retention_prompt(q, k, kd, v, sg, gn, batch, seq, off_meta, layer, n_layers, states):
    rows = q.shape[0]
    nc = seq // RET_CHUNK
    hg = RET_HEAD_GROUP
    wcols = hg * RET_DK
    dec, qd = _retention_tables(RET_CHUNK)
    qd_full = np.broadcast_to(qd[:, :, None], (RET_HEADS, RET_CHUNK, RET_DK)).copy()
    cd = np.broadcast_to(np.exp(_LOG_G * RET_CHUNK)[:, None, None], (RET_HEADS, 1, RET_DK)).astype(np.float32)
    assert off_meta % N_META == 0 and seq % RET_CHUNK == 0
    real = pl.BlockSpec((RET_CHUNK, wcols), lambda b, g, c: (b * nc + c, g))
    meta = pl.BlockSpec((N_META, wcols), lambda b, g, c: (off_meta // N_META + b, g))
    inputs = [q, k, kd, v, sg, q, k, kd, v, sg, gn.reshape(1, D_MODEL), jnp.asarray(dec), jnp.asarray(qd_full),
              jnp.asarray(cd)]
    in_specs = [real] * 5 + [meta] * 5 + [
        pl.BlockSpec((1, wcols), lambda b, g, c: (0, g)),
        pl.BlockSpec((hg, RET_CHUNK, RET_CHUNK), lambda b, g, c: (g, 0, 0)),
        pl.BlockSpec((hg, RET_CHUNK, RET_DK), lambda b, g, c: (g, 0, 0)),
        pl.BlockSpec((hg, 1, RET_DK), lambda b, g, c: (g, 0, 0))]
    body, aliases = _ret_prompt_kernel, {}
    if states is not None:
        body, aliases = _drop_ref(body, len(inputs)), {len(inputs): 2}
        inputs.append(states)
        in_specs.append(pl.BlockSpec(memory_space=pl.ANY))
    return pl.pallas_call(
        body, grid=(batch, RET_HEADS // hg, nc), in_specs=in_specs,
        out_specs=[real,
                   pl.BlockSpec((N_META, wcols), lambda b, g, c: (b, g)),
                   pl.BlockSpec((None, 1, hg, RET_DK, RET_DK), lambda b, g, c: (layer, b, g, 0, 0))],
        out_shape=[jax.ShapeDtypeStruct((rows, D_MODEL), MXU_DTYPE),
                   jax.ShapeDtypeStruct((batch * N_META, D_MODEL), MXU_DTYPE),
                   jax.ShapeDtypeStruct((n_layers, batch, RET_HEADS, RET_DK, RET_DK), F32)],
        input_output_aliases=aliases, compiler_params=_cparams(3), name="retention_prompt",
    )(*inputs)


def _ret_sample_kernel(q_ref, k_ref, kd_ref, v_ref, sg_ref, s0_ref, gn_ref, dec_ref, qd_ref, cd_ref,
                       z_ref, st_ref):
    for h in range(RET_HEADS):
        cs = slice(h * RET_DK, (h + 1) * RET_DK)
        q, k, kd, v = q_ref[:, cs], k_ref[:, cs], kd_ref[:, cs], v_ref[:, cs]
        s = s0_ref[h]
        scores = _mm_nt(q, k) * dec_ref[h]
        o = _mm(scores, v) + _mm(q, s) * qd_ref[h]
        z_ref[:, cs] = _group_norm_gate(o, sg_ref[:, cs], gn_ref[:, cs]).astype(z_ref.dtype)
        st_ref[h] = cd_ref[h] * s + _mm_tn(kd, v)


def _retention_sample(q, k, kd, v, sg, gn, state, layer, dec_seq, states):
    db, pad_rows, _ = q.shape
    dec, qd = _retention_tables(dec_seq)
    dec_p = np.zeros((RET_HEADS, pad_rows, pad_rows), np.float32)
    dec_p[:, :dec_seq, :dec_seq] = dec
    qd_p = np.zeros((RET_HEADS, pad_rows, RET_DK), np.float32)
    qd_p[:, :dec_seq, :] = qd[:, :, None]
    cd = np.broadcast_to(np.exp(_LOG_G * dec_seq)[:, None, None], (RET_HEADS, 1, RET_DK)).astype(np.float32)
    tok = pl.BlockSpec((None, pad_rows, D_MODEL), lambda b: (b, 0, 0))
    st = pl.BlockSpec((None, None, RET_HEADS, RET_DK, RET_DK), lambda b: (layer, b, 0, 0, 0))
    inputs = [q, k, kd, v, sg, state, gn.reshape(1, D_MODEL), jnp.asarray(dec_p), jnp.asarray(qd_p), jnp.asarray(cd)]
    in_specs = [tok] * 5 + [
        st, pl.BlockSpec((1, D_MODEL), lambda b: (0, 0)),
        pl.BlockSpec((RET_HEADS, pad_rows, pad_rows), lambda b: (0, 0, 0)),
        pl.BlockSpec((RET_HEADS, pad_rows, RET_DK), lambda b: (0, 0, 0)),
        pl.BlockSpec((RET_HEADS, 1, RET_DK), lambda b: (0, 0, 0))]
    body, aliases = _ret_sample_kernel, {}
    if states is not None:
        body, aliases = _drop_ref(body, len(inputs)), {len(inputs): 1}
        inputs.append(states)
        in_specs.append(pl.BlockSpec(memory_space=pl.ANY))
    return pl.pallas_call(
        body, grid=(db,), in_specs=in_specs, out_specs=[tok, st],
        out_shape=[jax.ShapeDtypeStruct((db, pad_rows, D_MODEL), MXU_DTYPE),
                   jax.ShapeDtypeStruct(state.shape, F32)],
        input_output_aliases=aliases, compiler_params=_cparams(1), name="retention_sample",
    )(*inputs)


def _shift_rows(x, prev8, j):
    xr = pltpu.roll(x, j, 0)
    pr = pltpu.roll(prev8, j, 0)
    row = lax.broadcasted_iota(jnp.int32, prev8.shape, 0)
    first = jnp.where(row < j, pr, xr[:V7X_SUBLANES])
    if x.shape[0] == V7X_SUBLANES:
        return first
    return jnp.concatenate([first, xr[V7X_SUBLANES:]], axis=0)


def _lru_gates(xc, wa, wx, ba, bx, cneg):
    r = jax.nn.sigmoid(_mm(xc, wa) + ba)
    i = jax.nn.sigmoid(_mm(xc, wx) + bx)
    log_a = cneg * r
    a = jnp.exp(log_a)
    return a, jnp.sqrt(1.0 - a * a) * (i * xc)


def _lru_tile(x, gate, hprev, xs_ref, sa_ref, sb_ref, cw, cb, wa, wx, ba, bx, cneg):
    n = x.shape[0]
    o = V7X_SUBLANES
    xs_ref[o:o + n, :] = x
    xc = cb + cw[3:4] * x
    for j in range(1, CONV_W):
        xc = xc + cw[3 - j:4 - j] * xs_ref[o - j:o - j + n, :]
    xs_ref[0:o, :] = x[n - o:]
    a, bb = _lru_gates(xc, wa, wx, ba, bx, cneg)
    row = lax.broadcasted_iota(jnp.int32, x.shape, 0) & (V7X_SUBLANES - 1)
    for s in (1, 2, 4):
        sa_ref[o:o + n, :] = a
        sb_ref[o:o + n, :] = bb
        m = row >= s
        a_s = jnp.where(m, sa_ref[o - s:o - s + n, :], 1.0)
        b_s = jnp.where(m, sb_ref[o - s:o - s + n, :], 0.0)
        bb = bb + a * b_s
        a = a * a_s
    h = hprev
    for g in range(n // V7X_SUBLANES):
        rs = slice(g * V7X_SUBLANES, (g + 1) * V7X_SUBLANES)
        hg = bb[rs] + a[rs] * h
        sb_ref[o + g * V7X_SUBLANES:o + (g + 1) * V7X_SUBLANES, :] = hg
        h = hg[V7X_SUBLANES - 1:V7X_SUBLANES]
    return gate * sb_ref[o:o + n, :], h


def _lru_prompt_kernel(x_ref, gate_ref, xm_ref, gatem_ref, cw_ref, cb_ref, wa_ref, wx_ref, ba_ref, bx_ref,
                       cneg_ref, y_ref, ym_ref, hl_ref, h_ref, xs_ref, sa_ref, sb_ref):
    c = pl.program_id(2)
    scratch = (xs_ref, sa_ref, sb_ref)
    params = (cw_ref[...], cb_ref[...], wa_ref[...], wx_ref[...], ba_ref[...], bx_ref[...], cneg_ref[...])

    @pl.when(c == 0)
    def _():
        zeros8 = jnp.zeros(h_ref.shape, F32)
        for ref in scratch:
            ref[0:V7X_SUBLANES, :] = zeros8
        y, h = _lru_tile(xm_ref[...], gatem_ref[...], zeros8[0:1], *scratch, *params)
        ym_ref[...] = y.astype(ym_ref.dtype)
        h_ref[...] = jnp.broadcast_to(h, h_ref.shape)

    y, h = _lru_tile(x_ref[...], gate_ref[...], h_ref[0:1, :], *scratch, *params)
    y_ref[...] = y.astype(y_ref.dtype)
    h_ref[...] = jnp.broadcast_to(h, h_ref.shape)
    hl_ref[...] = h


def _lru_param_specs(imap):
    cg = LRU_GROUP
    return [pl.BlockSpec((CONV_W, cg), lambda *a: (0, imap(*a))),
            pl.BlockSpec((1, cg), lambda *a: (0, imap(*a))),
            pl.BlockSpec((None, cg, cg), lambda *a: (imap(*a), 0, 0)),
            pl.BlockSpec((None, cg, cg), lambda *a: (imap(*a), 0, 0)),
            pl.BlockSpec((1, cg), lambda *a: (0, imap(*a))),
            pl.BlockSpec((1, cg), lambda *a: (0, imap(*a))),
            pl.BlockSpec((1, cg), lambda *a: (0, imap(*a)))]


def _lru_prompt(xb, gate, params, batch, seq, off_meta):
    rows = xb.shape[0]
    cg = LRU_GROUP
    nt = seq // SEQ_TILE
    assert seq % SEQ_TILE == 0 and off_meta % N_META == 0
    real = pl.BlockSpec((SEQ_TILE, cg), lambda g, b, c: (b * nt + c, g))
    meta = pl.BlockSpec((N_META, cg), lambda g, b, c: (off_meta // N_META + b, g))
    return pl.pallas_call(
        _lru_prompt_kernel, grid=(D_RNN // cg, batch, nt),
        in_specs=[real, real, meta, meta] + _lru_param_specs(lambda g, b, c: g),
        out_specs=[real,
                   pl.BlockSpec((N_META, cg), lambda g, b, c: (b, g)),
                   pl.BlockSpec((None, 1, cg), lambda g, b, c: (b, 0, g))],
        out_shape=[jax.ShapeDtypeStruct((rows, D_RNN), MXU_DTYPE),
                   jax.ShapeDtypeStruct((batch * N_META, D_RNN), MXU_DTYPE),
                   jax.ShapeDtypeStruct((batch, 1, D_RNN), F32)],
        scratch_shapes=[pltpu.VMEM((V7X_SUBLANES, cg), F32)]
        + [pltpu.VMEM((V7X_SUBLANES + SEQ_TILE, cg), F32)] * 3,
        compiler_params=_cparams(3), name="rglru_prompt",
    )(xb, gate, xb, gate, *params)


def _lru_sample_kernel(x_ref, gate_ref, c0_ref, h0_ref, cw_ref, cb_ref, wa_ref, wx_ref, ba_ref, bx_ref,
                       cneg_ref, y_ref, hl_ref):
    ds = x_ref.shape[0]
    cw, cb = cw_ref[...], cb_ref[...]
    xp = [c0_ref[j] for j in range(CONV_W - 1)] + [x_ref[t] for t in range(ds)]
    xcs = []
    for t in range(ds):
        xc = cb + cw[0:1] * xp[t]
        for j in range(1, CONV_W):
            xc = xc + cw[j:j + 1] * xp[t + j]
        xcs.append(xc)
    xc = jnp.concatenate(xcs, axis=0)
    a, bb = _lru_gates(xc, wa_ref[...], wx_ref[...], ba_ref[...], bx_ref[...], cneg_ref[...])
    db = x_ref.shape[1]
    h = h0_ref[...]
    for t in range(ds):
        rs = slice(t * db, (t + 1) * db)
        h = a[rs] * h + bb[rs]
        y_ref[t] = (gate_ref[t] * h).astype(y_ref.dtype)
    hl_ref[...] = h


def _lru_sample(xb, gate, conv0, h0, params):
    ds, db, _ = xb.shape
    cg = LRU_GROUP
    tok = pl.BlockSpec((ds, db, cg), lambda g: (0, 0, g))
    return pl.pallas_call(
        _lru_sample_kernel, grid=(D_RNN // cg,),
        in_specs=[tok, tok,
                  pl.BlockSpec((CONV_W - 1, db, cg), lambda g: (0, 0, g)),
                  pl.BlockSpec((db, cg), lambda g: (0, g))] + _lru_param_specs(lambda g: g),
        out_specs=[tok, pl.BlockSpec((db, cg), lambda g: (0, g))],
        out_shape=[jax.ShapeDtypeStruct((ds, db, D_RNN), MXU_DTYPE),
                   jax.ShapeDtypeStruct((db, D_RNN), F32)],
        compiler_params=_cparams(1), name="rglru_sample",
    )(xb, gate, conv0, h0, *params)


def _pool_project(means, x, w_ref, scale, g, b):
    ys = []
    for gi in range(len(POOL_WINDOWS)):
        cs = slice(gi * POOL_GROUP, (gi + 1) * POOL_GROUP)
        ys.append(_mm(means[gi] - x[:, cs], w_ref[gi]))
    y = jnp.concatenate(ys, axis=1) * scale
    return _layer_norm_rows(ALPHA * x + y, g, b)


def _pool_tile(x, carry, pos0, w_ref, scale, g, b):
    n = x.shape[0]
    pg = POOL_GROUP

    def shift(v, prev8, k):
        if k < V7X_SUBLANES:
            return _shift_rows(v, prev8, k)
        if n == V7X_SUBLANES:
            return prev8
        return jnp.concatenate([prev8, v[:n - V7X_SUBLANES]], axis=0)

    levels = [x]
    for li, k in enumerate((1, 2, 4, 8)):
        v = levels[-1][:, pg:] if li > 0 else levels[-1]
        levels.append(v + shift(v, carry[li][:, D_MODEL - v.shape[1]:], k))
    pos = (pos0 + lax.broadcasted_iota(jnp.int32, (n, 1), 0)).astype(F32)
    means = []
    for gi, w in enumerate(POOL_WINDOWS):
        cnt = jnp.minimum(float(w), pos + 1.0)
        means.append(levels[gi + 1][:, :pg] / cnt)
    out = _pool_project(means, x, w_ref, scale, g, b)
    new_carry = []
    for li in range(4):
        v = levels[li]
        tail = v[n - V7X_SUBLANES:]
        if v.shape[1] < D_MODEL:
            tail = jnp.concatenate([jnp.zeros((V7X_SUBLANES, D_MODEL - v.shape[1]), F32), tail], axis=1)
        new_carry.append(tail)
    return out, new_carry


def _pool_prompt_kernel(x_ref, xm_ref, w_ref, sc_ref, g_ref, b_ref, o_ref, obf_ref, om_ref, ombf_ref,
                        carry_ref):
    c = pl.program_id(1)
    consts = (w_ref, sc_ref[...], g_ref[...], b_ref[...])

    @pl.when(c == 0)
    def _():
        zero = jnp.zeros((V7X_SUBLANES, D_MODEL), F32)
        out, carry = _pool_tile(xm_ref[...], [zero] * 4, 0, *consts)
        om_ref[...] = out
        ombf_ref[...] = out.astype(ombf_ref.dtype)
        for li in range(4):
            carry_ref[li] = carry[li]

    out, carry = _pool_tile(x_ref[...], [carry_ref[li] for li in range(4)],
                            N_META + c * x_ref.shape[0], *consts)
    o_ref[...] = out
    obf_ref[...] = out.astype(obf_ref.dtype)
    for li in range(4):
        carry_ref[li] = carry[li]


def _pool_prompt(x, w, scale, g, b, batch, seq, off_meta):
    rows = x.shape[0]
    nt = seq // SEQ_TILE
    ng = len(POOL_WINDOWS)
    assert seq % SEQ_TILE == 0 and off_meta % N_META == 0
    real = pl.BlockSpec((SEQ_TILE, D_MODEL), lambda bb, c: (bb * nt + c, 0))
    vec = pl.BlockSpec((1, D_MODEL), lambda bb, c: (0, 0))
    meta_out = pl.BlockSpec((N_META, D_MODEL), lambda bb, c: (bb, 0))
    return pl.pallas_call(
        _pool_prompt_kernel, grid=(batch, nt),
        in_specs=[real, pl.BlockSpec((N_META, D_MODEL), lambda bb, c: (off_meta // N_META + bb, 0)),
                  pl.BlockSpec((ng, POOL_GROUP, POOL_GROUP), lambda bb, c: (0, 0, 0)), vec, vec, vec],
        out_specs=[real, real, meta_out, meta_out],
        out_shape=[jax.ShapeDtypeStruct((rows, D_MODEL), F32), jax.ShapeDtypeStruct((rows, D_MODEL), MXU_DTYPE),
                   jax.ShapeDtypeStruct((batch * N_META, D_MODEL), F32),
                   jax.ShapeDtypeStruct((batch * N_META, D_MODEL), MXU_DTYPE)],
        scratch_shapes=[pltpu.VMEM((4, V7X_SUBLANES, D_MODEL), F32)],
        compiler_params=_cparams(2), name="pool_prompt",
    )(x, x, w, scale.reshape(1, D_MODEL), g.reshape(1, D_MODEL), b.reshape(1, D_MODEL))


def _pool_sample_kernel(x_ref, buf_ref, w_ref, sc_ref, g_ref, b_ref, o_ref, obf_ref, *, pos0):
    ds = x_ref.shape[0]
    nbuf = buf_ref.shape[0]
    pg = POOL_GROUP
    means = []
    for gi, w in enumerate(POOL_WINDOWS):
        cs = slice(gi * pg, (gi + 1) * pg)
        xp = [buf_ref[j, :, cs] for j in range(nbuf - (w - 1), nbuf)] + [x_ref[t, :, cs] for t in range(ds)]
        rows = []
        for t in range(ds):
            acc = xp[t]
            for j in range(1, w):
                acc = acc + xp[t + j]
            rows.append(acc / float(min(w, pos0 + t + 1)))
        means.append(jnp.concatenate(rows, axis=0))
    x = jnp.concatenate([x_ref[t] for t in range(ds)], axis=0)
    out = _pool_project(means, x, w_ref, sc_ref[...], g_ref[...], b_ref[...])
    bs = x_ref.shape[1]
    for t in range(ds):
        o_ref[t] = out[t * bs:(t + 1) * bs]
        obf_ref[t] = out[t * bs:(t + 1) * bs].astype(obf_ref.dtype)


def _pool_sample(x, buf, w, scale, g, b, pos0):
    ds, db, _ = x.shape
    nbuf = buf.shape[0]
    bs = min(POOL_SEQ_BLOCK, db)
    ng = len(POOL_WINDOWS)
    assert db % bs == 0
    tok = pl.BlockSpec((ds, bs, D_MODEL), lambda s: (0, s, 0))
    vec = pl.BlockSpec((1, D_MODEL), lambda s: (0, 0))
    return pl.pallas_call(
        functools.partial(_pool_sample_kernel, pos0=pos0), grid=(db // bs,),
        in_specs=[tok, pl.BlockSpec((nbuf, bs, D_MODEL), lambda s: (0, s, 0)),
                  pl.BlockSpec((ng, POOL_GROUP, POOL_GROUP), lambda s: (0, 0, 0)), vec, vec, vec],
        out_specs=[tok, tok],
        out_shape=[jax.ShapeDtypeStruct((ds, db, D_MODEL), F32), jax.ShapeDtypeStruct((ds, db, D_MODEL), MXU_DTYPE)],
        compiler_params=_cparams(1), name="pool_sample",
    )(x, buf, w, scale.reshape(1, D_MODEL), g.reshape(1, D_MODEL), b.reshape(1, D_MODEL))


def _block_diag_groups(w):
    per = LRU_GROUP // LRU_BLOCK
    w = w.reshape(LRU_BLOCKS // per, per, LRU_BLOCK, LRU_BLOCK)
    eye = jnp.eye(per, dtype=w.dtype)
    dense = jnp.einsum('gpcd,pq->gpcqd', w, eye)
    return dense.reshape(LRU_BLOCKS // per, LRU_GROUP, LRU_GROUP)


def kernel(x_prompt, x_sample, state_ret, state_lru_h, state_lru_conv, state_pool, meta_tokens,
           ret_w_in, ret_gn_g, ret_w_out, lru_w_in, lru_conv_w, lru_conv_b, lru_w_a, lru_b_a,
           lru_w_x, lru_b_x, lru_lambda, lru_w_out, pool_w, pool_scale, ln_mix_g, ln_mix_b,
           ln_ffn_g, ln_ffn_b, ffn_w1, ffn_w3, ffn_w2):
    batch, seq, d = x_prompt.shape
    db, ds, _ = x_sample.shape
    assert d == D_MODEL
    n_real, n_samp, n_meta = batch * seq, db * ds, batch * N_META
    off_s, off_m = n_real, n_real + n_samp
    mx = MXU_DTYPE

    x = jnp.concatenate([
        x_prompt.reshape(n_real, d),
        jnp.swapaxes(x_sample, 0, 1).reshape(n_samp, d),
        jnp.broadcast_to(meta_tokens[None].astype(x_prompt.dtype), (batch, N_META, d)).reshape(n_meta, d),
    ], axis=0)
    xbf = x.astype(mx)

    t_real = jnp.tile(jnp.arange(seq, dtype=F32), batch)
    t_samp = jnp.repeat(jnp.arange(ds, dtype=F32), db)
    t_meta = jnp.tile(jnp.arange(N_META, dtype=F32), batch)
    pos = jnp.concatenate([N_META + t_real, PAST_LEN + t_samp, t_meta])
    half = RET_DK // 2
    inv = ROPE_BASE ** (-jnp.arange(half, dtype=F32) / half)
    ang = pos[:, None] * inv[None, :]
    cos_t, sin_t = jnp.cos(ang), jnp.sin(ang)
    e_rows = jnp.concatenate([
        (RET_CHUNK - 1.0) - jnp.mod(t_real, float(RET_CHUNK)), (ds - 1.0) - t_samp, (N_META - 1.0) - t_meta,
    ])[:, None]
    lg_cols = jnp.repeat(jnp.asarray(_LOG_G, F32), RET_DK)[None, :]

    def sample_rows(a):
        return a[off_s:off_m].reshape(ds, db, a.shape[1])

    def merge(full, samp, meta):
        full = lax.dynamic_update_slice(full, samp.reshape(n_samp, samp.shape[-1]), (off_s, 0))
        return lax.dynamic_update_slice(full, meta, (off_m, 0))

    pad_rows = -(-ds // V7X_BF16_ROWS) * V7X_BF16_ROWS

    def to_seq_major(a):
        a = jnp.swapaxes(a, 0, 1)
        return jnp.pad(a, ((0, 0), (0, pad_rows - ds), (0, 0)))

    def last_rows(a, n):
        return jnp.stack([a[(b + 1) * seq - n:(b + 1) * seq] for b in range(batch)])

    ret_w_in_mx, ret_w_out_mx = ret_w_in.astype(mx), ret_w_out.astype(mx)
    lru_w_in_mx, lru_w_out_mx = lru_w_in.astype(mx), lru_w_out.astype(mx)
    ff_pad = D_FF_PAD - D_FF
    w1_mx = jnp.pad(ffn_w1.astype(mx), ((0, 0), (0, 0), (0, ff_pad)))
    w3_mx = jnp.pad(ffn_w3.astype(mx), ((0, 0), (0, 0), (0, ff_pad)))
    w2_mx = jnp.pad(ffn_w2.astype(mx), ((0, 0), (0, ff_pad), (0, 0)))

    n_ret = ret_w_in.shape[0]
    ret_p = ret_s = None
    outs_h_p, outs_h_s, outs_conv_p, outs_conv_s, outs_pool_p, outs_pool_s = [], [], [], [], [], []

    for i in range(DEPTH):
        j, kind = i // 3, i % 3
        if kind == 0:
            (q,) = _project(_proj_q_kernel, xbf, [ret_w_in_mx], j, [0], D_MODEL, 512, [mx],
                            row_inputs=[cos_t, sin_t], name="ret_q")
            k, kd = _project(_proj_k_kernel, xbf, [ret_w_in_mx], j, [D_MODEL], D_MODEL, 512, [mx, mx],
                             row_inputs=[cos_t, sin_t, e_rows], col_inputs=[lg_cols], name="ret_k")
            (v,) = _project(_proj_cast_kernel, xbf, [ret_w_in_mx], j, [2 * D_MODEL], D_MODEL, 512, [mx],
                            name="ret_v")
            (sg,) = _project(_proj_silu_kernel, xbf, [ret_w_in_mx], j, [3 * D_MODEL], D_MODEL, 512, [F32],
                             name="ret_g")
            z, z_meta, ret_p = _retention_prompt(q, k, kd, v, sg, ret_gn_g[j], batch, seq, off_m, j, n_ret, ret_p)
            z_s, ret_s = _retention_sample(*(to_seq_major(sample_rows(a)) for a in (q, k, kd, v, sg)),
                                           ret_gn_g[j], state_ret, j, ds, ret_s)
            z = merge(z, jnp.swapaxes(z_s[:, :ds], 0, 1), z_meta)
            x, xbf = _residual_ln(z, ret_w_out_mx, j, x, ln_mix_g[i], ln_mix_b[i], 1024, name="ret_out")
        elif kind == 1:
            (gate,) = _project(_proj_gelu_kernel, xbf, [lru_w_in_mx], j, [0], D_RNN, 512, [F32], name="lru_gate")
            (xb,) = _project(_proj_cast_kernel, xbf, [lru_w_in_mx], j, [D_RNN], D_RNN, 512, [F32], name="lru_x")
            params = (lru_conv_w[j], lru_conv_b[j].reshape(1, D_RNN),
                      _block_diag_groups(lru_w_a[j]).astype(mx), _block_diag_groups(lru_w_x[j]).astype(mx),
                      lru_b_a[j].reshape(1, D_RNN), lru_b_x[j].reshape(1, D_RNN),
                      (-LRU_C * jax.nn.softplus(-lru_lambda[j].astype(F32))).reshape(1, D_RNN))
            y, y_meta, h_p = _lru_prompt(xb, gate, params, batch, seq, off_m)
            xb_s = sample_rows(xb)
            y_s, h_s = _lru_sample(xb_s, sample_rows(gate), jnp.swapaxes(state_lru_conv[j], 0, 1).astype(F32),
                                   state_lru_h[j].astype(F32), params)
            y = merge(y, y_s, y_meta)
            x, xbf = _residual_ln(y, lru_w_out_mx, j, x, ln_mix_g[i], ln_mix_b[i], 512, name="lru_out")
            outs_h_p.append(h_p.reshape(batch, D_RNN))
            outs_h_s.append(h_s)
            assert seq >= CONV_W - 1
            outs_conv_p.append(last_rows(xb, CONV_W - 1))
            xp_s = jnp.concatenate([state_lru_conv[j].astype(F32), jnp.swapaxes(xb_s, 0, 1)], axis=1)
            outs_conv_s.append(xp_s[:, -(CONV_W - 1):])
        else:
            p = POOL_MAX - 1
            w_pool = pool_w[j].astype(mx)
            x_s = sample_rows(x)
            assert seq >= p
            outs_pool_p.append(last_rows(x, p))
            xp_s = jnp.concatenate([state_pool[j].astype(F32), jnp.swapaxes(x_s, 0, 1)], axis=1)
            outs_pool_s.append(xp_s[:, -p:])
            xo, xobf, xm, xmbf = _pool_prompt(x, w_pool, pool_scale[j], ln_mix_g[i], ln_mix_b[i], batch, seq, off_m)
            buf_tm = jnp.swapaxes(state_pool[j].astype(F32), 0, 1)
            xs, xsbf = _pool_sample(x_s, buf_tm, w_pool, pool_scale[j], ln_mix_g[i], ln_mix_b[i], PAST_LEN)
            x, xbf = merge(xo, xs, xm), merge(xobf, xsbf, xmbf)
        (hid,) = _project(_proj_swiglu_kernel, xbf, [w1_mx, w3_mx], i, [0, 0], D_FF_PAD, 512, [mx], name="ffn_up")
        x, xbf = _residual_ln(hid, w2_mx, i, x, ln_ffn_g[i], ln_ffn_b[i], 1024, name="ffn_down")

    y_prompt = x[:n_real].reshape(batch, seq, d)
    y_sample = jnp.swapaxes(x[off_s:off_m].reshape(ds, db, d), 0, 1)
    return (y_prompt, y_sample, ret_p, ret_s, jnp.stack(outs_h_p),
            jnp.stack(outs_h_s), jnp.stack(outs_conv_p), jnp.stack(outs_conv_s), jnp.stack(outs_pool_p),
            jnp.stack(outs_pool_s))
```

```python
import functools

import numpy as np
import jax
import jax.numpy as jnp
from jax import lax
from jax.experimental import pallas as pl
from jax.experimental.pallas import tpu as pltpu

F32 = jnp.float32
MXU_DTYPE = jnp.bfloat16

D_MODEL = 4096
N_META = 16
PAST_LEN = 16384
RET_HEADS = 16
RET_DK = D_MODEL // RET_HEADS
RET_CHUNK = 128
ROPE_BASE = 10000.0
D_RNN = 5632
LRU_BLOCKS = 16
LRU_BLOCK = D_RNN // LRU_BLOCKS
CONV_W = 4
LRU_C = 8.0
POOL_WINDOWS = (2, 4, 8, 16)
POOL_GROUP = D_MODEL // len(POOL_WINDOWS)
POOL_MAX = 16
D_FF = 11008
DEPTH = 4
ALPHA = (2.0 * DEPTH) ** 0.25
LN_EPS = 1e-5

V7X_SUBLANES = 8
V7X_LANES = 128
V7X_BF16_ROWS = 16
V7X_VMEM_LIMIT = 58 * 2 ** 20

ROW_TILE = 1104
FFN_ROW_TILE = 592
ACC_ROW_TILE = 800
ACC_COL_CHUNK = 512
ACC_ROW_CHUNK = 16
SEQ_TILE = 256
RET_HEAD_GROUP = 16
LRU_GROUP = 4 * LRU_BLOCK
POOL_SEQ_BLOCK = 32

_LOG_G = np.log1p(-np.exp2(-5.0 - np.arange(RET_HEADS, dtype=np.float64)))


def _cparams(n_axes):
    return pltpu.CompilerParams(dimension_semantics=("arbitrary",) * n_axes,
                                vmem_limit_bytes=V7X_VMEM_LIMIT)


def _mm(a, b):
    return jnp.dot(a.astype(MXU_DTYPE), b.astype(MXU_DTYPE), preferred_element_type=F32)


def _mm_nt(a, b):
    return lax.dot_general(a.astype(MXU_DTYPE), b.astype(MXU_DTYPE), (((1,), (1,)), ((), ())),
                           preferred_element_type=F32)


def _mm_tn(a, b):
    return lax.dot_general(a.astype(MXU_DTYPE), b.astype(MXU_DTYPE), (((0,), (0,)), ((), ())),
                           preferred_element_type=F32)


def _layer_norm_rows(y, g, b):
    mu = jnp.mean(y, axis=-1, keepdims=True)
    d = y - mu
    var = jnp.mean(d * d, axis=-1, keepdims=True)
    return d * lax.rsqrt(var + LN_EPS) * g + b


def _rope_cols(acc, cos, sin):
    half = RET_DK // 2
    parts = []
    for h in range(acc.shape[1] // RET_DK):
        x1 = acc[:, h * RET_DK: h * RET_DK + half]
        x2 = acc[:, h * RET_DK + half: (h + 1) * RET_DK]
        parts.append(x1 * cos - x2 * sin)
        parts.append(x1 * sin + x2 * cos)
    return jnp.concatenate(parts, axis=1)


def _epi_q(accs, cos_ref, sin_ref, o_ref):
    o_ref[...] = _rope_cols(accs[0], cos_ref[...], sin_ref[...]).astype(o_ref.dtype)


def _epi_k(accs, cos_ref, sin_ref, e_ref, lg_ref, k_ref, kd_ref):
    k = _rope_cols(accs[0], cos_ref[...], sin_ref[...]) * (RET_DK ** -0.5)
    k_ref[...] = k.astype(k_ref.dtype)
    kd_ref[...] = (k * jnp.exp(e_ref[...] * lg_ref[...])).astype(kd_ref.dtype)


def _epi_cast(accs, o_ref):
    o_ref[...] = accs[0].astype(o_ref.dtype)


def _epi_silu(accs, o_ref):
    o_ref[...] = (accs[0] * jax.nn.sigmoid(accs[0])).astype(o_ref.dtype)


def _epi_gelu(accs, o_ref):
    o_ref[...] = jax.nn.gelu(accs[0]).astype(o_ref.dtype)


def _epi_swiglu(accs, o_ref):
    a, b = accs
    o_ref[...] = (a * jax.nn.sigmoid(a) * b).astype(o_ref.dtype)


def _project_kernel(x_ref, *refs, epilogue, n_w):
    w_refs, rest, w_mx = refs[:n_w], refs[n_w:len(refs) - n_w], refs[len(refs) - n_w:]

    @pl.when(pl.program_id(1) == 0)
    def _():
        for w_ref, s_ref in zip(w_refs, w_mx):
            s_ref[...] = w_ref[...].astype(s_ref.dtype)

    x = x_ref[...]
    epilogue([jnp.dot(x, s_ref[...], preferred_element_type=F32) for s_ref in w_mx], *rest)


def _project(epilogue, x, weights, layer, col_offsets, n_cols, bn, bm, out_dtypes, row_inputs=(),
             col_inputs=(), name=None):
    rows, k = x.shape
    bm = min(bm, -(-rows // V7X_BF16_ROWS) * V7X_BF16_ROWS)
    assert all(off % bn == 0 for off in col_offsets)
    grid = (pl.cdiv(n_cols, bn), pl.cdiv(rows, bm))
    in_specs = [pl.BlockSpec((bm, k), lambda j, i: (i, 0))]
    for off in col_offsets:
        in_specs.append(pl.BlockSpec((None, k, bn),
                                     functools.partial(lambda j, i, o: (layer, 0, j + o), o=off // bn)))
    for r in row_inputs:
        in_specs.append(pl.BlockSpec((bm, r.shape[1]), lambda j, i: (i, 0)))
    for c in col_inputs:
        in_specs.append(pl.BlockSpec((1, bn), lambda j, i: (0, j)))
    out_shape = [jax.ShapeDtypeStruct((rows, n_cols), dt) for dt in out_dtypes]
    out_specs = [pl.BlockSpec((bm, bn), lambda j, i: (i, j)) for _ in out_dtypes]
    return pl.pallas_call(
        functools.partial(_project_kernel, epilogue=epilogue, n_w=len(weights)),
        grid=grid, in_specs=in_specs, out_specs=out_specs, out_shape=out_shape,
        scratch_shapes=[pltpu.VMEM((k, bn), MXU_DTYPE) for _ in weights],
        compiler_params=_cparams(2), name=name,
    )(x, *weights, *row_inputs, *col_inputs)


def _residual_ln_kernel(a_ref, w_ref, x_hbm, g_ref, b_ref, o_ref, obf_ref, xbuf, sem, *, rows, kdim):
    i, k = pl.program_id(0), pl.program_id(1)
    n_tiles, n_k = pl.num_programs(0), pl.num_programs(1)
    bm, d = o_ref.shape
    tail = rows - (pl.cdiv(rows, bm) - 1) * bm

    def with_x_copy(fn):
        def copy(size):
            return pltpu.make_async_copy(
                x_hbm.at[pl.ds(pl.multiple_of(i * bm, V7X_BF16_ROWS), size), :],
                xbuf.at[pl.ds(0, size), :], sem)
        if tail == bm:
            fn(copy(bm))
        else:
            pl.when(i < n_tiles - 1)(lambda: fn(copy(bm)))
            pl.when(i == n_tiles - 1)(lambda: fn(copy(tail)))

    def accumulate(first, k_valid):
        a = a_ref[...]
        if k_valid < a.shape[1]:
            a = jnp.where(lax.broadcasted_iota(jnp.int32, a.shape, 1) < k_valid, a, jnp.zeros_like(a))
        for n in range(d // ACC_COL_CHUNK):
            cs = slice(n * ACC_COL_CHUNK, (n + 1) * ACC_COL_CHUNK)
            w = w_ref[:, cs]
            if k_valid < a.shape[1]:
                w = jnp.where(lax.broadcasted_iota(jnp.int32, w.shape, 0) < k_valid, w, jnp.zeros_like(w))
            p = _mm(a, w)
            o_ref[:, cs] = p if first else o_ref[:, cs] + p

    bk = a_ref.shape[1]
    k_tail = kdim - (pl.cdiv(kdim, bk) - 1) * bk

    @pl.when(k == 0)
    def _():
        with_x_copy(lambda cp: cp.start())
        accumulate(True, bk)

    @pl.when(jnp.logical_and(k > 0, k < n_k - 1))
    def _():
        accumulate(False, bk)

    @pl.when(jnp.logical_and(k > 0, k == n_k - 1))
    def _():
        accumulate(False, k_tail)

    @pl.when(k == n_k - 1)
    def _():
        with_x_copy(lambda cp: cp.wait())
        rc = ACC_ROW_CHUNK
        for r in range(bm // rc):
            rs = slice(r * rc, (r + 1) * rc)
            out = _layer_norm_rows(ALPHA * xbuf[rs, :] + o_ref[rs, :], g_ref[...], b_ref[...])
            o_ref[rs, :] = out
            obf_ref[rs, :] = out.astype(obf_ref.dtype)


def _residual_ln(a, w, layer, x, g, b, bk, name=None):
    rows, kdim = a.shape
    d = w.shape[2]
    bm = min(ACC_ROW_TILE, -(-rows // V7X_BF16_ROWS) * V7X_BF16_ROWS)
    n_k = pl.cdiv(kdim, bk)
    assert bm % ACC_ROW_CHUNK == 0 and (n_k > 1 or kdim == bk)
    n_tiles = pl.cdiv(rows, bm)
    assert n_tiles > 1 or rows == bm
    out_spec = pl.BlockSpec((bm, d), lambda i, k: (i, 0), pipeline_mode=pl.Buffered(1))
    return pl.pallas_call(
        functools.partial(_residual_ln_kernel, rows=rows, kdim=kdim), grid=(n_tiles, n_k),
        in_specs=[pl.BlockSpec((bm, bk), lambda i, k: (i, k)),
                  pl.BlockSpec((None, bk, d), lambda i, k: (layer, k, 0)),
                  pl.BlockSpec(memory_space=pl.ANY),
                  pl.BlockSpec((1, d), lambda i, k: (0, 0)),
                  pl.BlockSpec((1, d), lambda i, k: (0, 0))],
        out_specs=[out_spec, out_spec],
        out_shape=[jax.ShapeDtypeStruct((rows, d), F32), jax.ShapeDtypeStruct((rows, d), MXU_DTYPE)],
        scratch_shapes=[pltpu.VMEM((bm, d), F32), pltpu.SemaphoreType.DMA],
        compiler_params=_cparams(2), name=name,
    )(a, w, x, g.reshape(1, d), b.reshape(1, d))


def _group_norm_gate(o, sg, gn):
    mu = jnp.mean(o, axis=-1, keepdims=True)
    d = o - mu
    var = jnp.mean(d * d, axis=-1, keepdims=True)
    return sg * (d * lax.rsqrt(var + LN_EPS) * gn)


def _ret_prompt_kernel(q_ref, k_ref, kd_ref, v_ref, sg_ref,
                       qm_ref, km_ref, kdm_ref, vm_ref, sgm_ref,
                       gn_ref, dec_ref, qd_ref, cd_ref, z_ref, zm_ref, st_ref):
    c = pl.program_id(2)
    heads = dec_ref.shape[0]

    @pl.when(c == 0)
    def _():
        for h in range(heads):
            cs = slice(h * RET_DK, (h + 1) * RET_DK)
            q, k, kd, v = qm_ref[:, cs], km_ref[:, cs], kdm_ref[:, cs], vm_ref[:, cs]
            scores = _mm_nt(q, k) * dec_ref[h, :N_META, :N_META]
            o = _mm(scores, v)
            zm_ref[:, cs] = _group_norm_gate(o, sgm_ref[:, cs], gn_ref[:, cs]).astype(zm_ref.dtype)
            st_ref[0, h] = _mm_tn(kd, v)

    for h in range(heads):
        cs = slice(h * RET_DK, (h + 1) * RET_DK)
        q, k, kd, v = q_ref[:, cs], k_ref[:, cs], kd_ref[:, cs], v_ref[:, cs]
        s = st_ref[0, h]
        scores = _mm_nt(q, k) * dec_ref[h]
        o = _mm(scores, v) + _mm(q, s) * qd_ref[h]
        z_ref[:, cs] = _group_norm_gate(o, sg_ref[:, cs], gn_ref[:, cs]).astype(z_ref.dtype)
        st_ref[0, h] = cd_ref[h] * s + _mm_tn(kd, v)


def _retention_tables(length):
    idx = np.arange(length, dtype=np.float64)
    diff = idx[:, None] - idx[None, :]
    dec = np.where(diff >= 0, np.exp(_LOG_G[:, None, None] * np.maximum(diff, 0.0)), 0.0)
    qd = np.exp(_LOG_G[:, None] * (idx[None, :] + 1.0))
    return dec.astype(np.float32), qd.astype(np.float32)


def _drop_ref(body, index):
    def wrapped(*refs):
        body(*refs[:index], *refs[index + 1:])
    return wrapped


def _retention_prompt(q, k, kd, v, sg, gn, batch, seq, off_meta, layer, n_layers, states):
    rows = q.shape[0]
    nc = seq // RET_CHUNK
    hg = RET_HEAD_GROUP
    wcols = hg * RET_DK
    dec, qd = _retention_tables(RET_CHUNK)
    qd_full = np.broadcast_to(qd[:, :, None], (RET_HEADS, RET_CHUNK, RET_DK)).copy()
    cd = np.broadcast_to(np.exp(_LOG_G * RET_CHUNK)[:, None, None], (RET_HEADS, 1, RET_DK)).astype(np.float32)
    assert off_meta % N_META == 0 and seq % RET_CHUNK == 0
    real = pl.BlockSpec((RET_CHUNK, wcols), lambda b, g, c: (b * nc + c, g))
    meta = pl.BlockSpec((N_META, wcols), lambda b, g, c: (off_meta // N_META + b, g))
    inputs = [q, k, kd, v, sg, q, k, kd, v, sg, gn.reshape(1, D_MODEL), jnp.asarray(dec), jnp.asarray(qd_full),
              jnp.asarray(cd)]
    in_specs = [real] * 5 + [meta] * 5 + [
        pl.BlockSpec((1, wcols), lambda b, g, c: (0, g)),
        pl.BlockSpec((hg, RET_CHUNK, RET_CHUNK), lambda b, g, c: (g, 0, 0)),
        pl.BlockSpec((hg, RET_CHUNK, RET_DK), lambda b, g, c: (g, 0, 0)),
        pl.BlockSpec((hg, 1, RET_DK), lambda b, g, c: (g, 0, 0))]
    body, aliases = _ret_prompt_kernel, {}
    if states is not None:
        body, aliases = _drop_ref(body, len(inputs)), {len(inputs): 2}
        inputs.append(states)
        in_specs.append(pl.BlockSpec(memory_space=pl.ANY))
    return pl.pallas_call(
        body, grid=(batch, RET_HEADS // hg, nc), in_specs=in_specs,
        out_specs=[real,
                   pl.BlockSpec((N_META, wcols), lambda b, g, c: (b, g)),
                   pl.BlockSpec((None, 1, hg, RET_DK, RET_DK), lambda b, g, c: (layer, b, g, 0, 0))],
        out_shape=[jax.ShapeDtypeStruct((rows, D_MODEL), MXU_DTYPE),
                   jax.ShapeDtypeStruct((batch * N_META, D_MODEL), MXU_DTYPE),
                   jax.ShapeDtypeStruct((n_layers, batch, RET_HEADS, RET_DK, RET_DK), F32)],
        input_output_aliases=aliases, compiler_params=_cparams(3), name="retention_prompt",
    )(*inputs)


def _ret_sample_kernel(q_ref, k_ref, kd_ref, v_ref, sg_ref, s0_ref, gn_ref, dec_ref, qd_ref, cd_ref,
                       z_ref, st_ref):
    for h in range(RET_HEADS):
        cs = slice(h * RET_DK, (h + 1) * RET_DK)
        q, k, kd, v = q_ref[:, cs], k_ref[:, cs], kd_ref[:, cs], v_ref[:, cs]
        s = s0_ref[h]
        scores = _mm_nt(q, k) * dec_ref[h]
        o = _mm(scores, v) + _mm(q, s) * qd_ref[h]
        z_ref[:, cs] = _group_norm_gate(o, sg_ref[:, cs], gn_ref[:, cs]).astype(z_ref.dtype)
        st_ref[h] = cd_ref[h] * s + _mm_tn(kd, v)


def _retention_sample(q, k, kd, v, sg, gn, state, layer, dec_seq, states):
    db, pad_rows, _ = q.shape
    dec, qd = _retention_tables(dec_seq)
    dec_p = np.zeros((RET_HEADS, pad_rows, pad_rows), np.float32)
    dec_p[:, :dec_seq, :dec_seq] = dec
    qd_p = np.zeros((RET_HEADS, pad_rows, RET_DK), np.float32)
    qd_p[:, :dec_seq, :] = qd[:, :, None]
    cd = np.broadcast_to(np.exp(_LOG_G * dec_seq)[:, None, None], (RET_HEADS, 1, RET_DK)).astype(np.float32)
    tok = pl.BlockSpec((None, pad_rows, D_MODEL), lambda b: (b, 0, 0))
    st = pl.BlockSpec((None, None, RET_HEADS, RET_DK, RET_DK), lambda b: (layer, b, 0, 0, 0))
    inputs = [q, k, kd, v, sg, state, gn.reshape(1, D_MODEL), jnp.asarray(dec_p), jnp.asarray(qd_p), jnp.asarray(cd)]
    in_specs = [tok] * 5 + [
        st, pl.BlockSpec((1, D_MODEL), lambda b: (0, 0)),
        pl.BlockSpec((RET_HEADS, pad_rows, pad_rows), lambda b: (0, 0, 0)),
        pl.BlockSpec((RET_HEADS, pad_rows, RET_DK), lambda b: (0, 0, 0)),
        pl.BlockSpec((RET_HEADS, 1, RET_DK), lambda b: (0, 0, 0))]
    body, aliases = _ret_sample_kernel, {}
    if states is not None:
        body, aliases = _drop_ref(body, len(inputs)), {len(inputs): 1}
        inputs.append(states)
        in_specs.append(pl.BlockSpec(memory_space=pl.ANY))
    return pl.pallas_call(
        body, grid=(db,), in_specs=in_specs, out_specs=[tok, st],
        out_shape=[jax.ShapeDtypeStruct((db, pad_rows, D_MODEL), MXU_DTYPE),
                   jax.ShapeDtypeStruct(state.shape, F32)],
        input_output_aliases=aliases, compiler_params=_cparams(1), name="retention_sample",
    )(*inputs)


def _shift_rows(x, prev8, j):
    xr = pltpu.roll(x, j, 0)
    pr = pltpu.roll(prev8, j, 0)
    row = lax.broadcasted_iota(jnp.int32, prev8.shape, 0)
    first = jnp.where(row < j, pr, xr[:V7X_SUBLANES])
    if x.shape[0] == V7X_SUBLANES:
        return first
    return jnp.concatenate([first, xr[V7X_SUBLANES:]], axis=0)


def _lru_gates(xc, wa, wx, ba, bx, cneg):
    r = jax.nn.sigmoid(_mm(xc, wa) + ba)
    i = jax.nn.sigmoid(_mm(xc, wx) + bx)
    log_a = cneg * r
    a = jnp.exp(log_a)
    z = 1.0 - a * a
    return a, jnp.where(z > 0.0, z * lax.rsqrt(z), 0.0) * (i * xc)


def _lru_tile(x, gate, hprev, xs_ref, sb_ref, cw, cb, wa, wx, ba, bx, cneg):
    n = x.shape[0]
    o = V7X_SUBLANES
    xs_ref[o:o + n, :] = x
    xc = cb + cw[3:4] * x
    for j in range(1, CONV_W):
        xc = xc + cw[3 - j:4 - j] * xs_ref[o - j:o - j + n, :]
    xs_ref[0:o, :] = x[n - o:]
    a, bb = _lru_gates(xc, wa, wx, ba, bx, cneg)
    groups = n // V7X_SUBLANES
    a = a.reshape(groups, V7X_SUBLANES, a.shape[1])
    bb = bb.reshape(groups, V7X_SUBLANES, bb.shape[1])
    row = lax.broadcasted_iota(jnp.int32, (1,) + a.shape[1:], 1)
    for s in (1, 2, 4):
        m = row >= s
        a_s = jnp.where(m, pltpu.roll(a, s, 1), 1.0)
        b_s = jnp.where(m, pltpu.roll(bb, s, 1), 0.0)
        bb = bb + a * b_s
        a = a * a_s
    h = hprev
    for g in range(groups):
        hg = bb[g] + a[g] * h
        sb_ref[o + g * V7X_SUBLANES:o + (g + 1) * V7X_SUBLANES, :] = hg
        h = hg[V7X_SUBLANES - 1:V7X_SUBLANES]
    return gate * sb_ref[o:o + n, :], h


def _lru_prompt_kernel(x_ref, gate_ref, xm_ref, gatem_ref, cw_ref, cb_ref, wa_ref, wx_ref, ba_ref, bx_ref,
                       cneg_ref, y_ref, ym_ref, hl_ref, h_ref, xs_ref, sb_ref):
    c = pl.program_id(2)
    scratch = (xs_ref, sb_ref)
    params = (cw_ref[...], cb_ref[...], wa_ref[...], wx_ref[...], ba_ref[...], bx_ref[...], cneg_ref[...])

    @pl.when(c == 0)
    def _():
        zeros8 = jnp.zeros(h_ref.shape, F32)
        for ref in scratch:
            ref[0:V7X_SUBLANES, :] = zeros8
        y, h = _lru_tile(xm_ref[...], gatem_ref[...], zeros8[0:1], *scratch, *params)
        ym_ref[...] = y.astype(ym_ref.dtype)
        h_ref[...] = jnp.broadcast_to(h, h_ref.shape)

    y, h = _lru_tile(x_ref[...], gate_ref[...], h_ref[0:1, :], *scratch, *params)
    y_ref[...] = y.astype(y_ref.dtype)
    h_ref[...] = jnp.broadcast_to(h, h_ref.shape)
    hl_ref[...] = h


def _lru_param_specs(imap):
    cg = LRU_GROUP
    return [pl.BlockSpec((CONV_W, cg), lambda *a: (0, imap(*a))),
            pl.BlockSpec((1, cg), lambda *a: (0, imap(*a))),
            pl.BlockSpec((None, cg, cg), lambda *a: (imap(*a), 0, 0)),
            pl.BlockSpec((None, cg, cg), lambda *a: (imap(*a), 0, 0)),
            pl.BlockSpec((1, cg), lambda *a: (0, imap(*a))),
            pl.BlockSpec((1, cg), lambda *a: (0, imap(*a))),
            pl.BlockSpec((1, cg), lambda *a: (0, imap(*a)))]


def _lru_prompt(xb, gate, params, batch, seq, off_meta):
    rows = xb.shape[0]
    cg = LRU_GROUP
    nt = seq // SEQ_TILE
    assert seq % SEQ_TILE == 0 and off_meta % N_META == 0
    real = pl.BlockSpec((SEQ_TILE, cg), lambda g, b, c: (b * nt + c, g))
    meta = pl.BlockSpec((N_META, cg), lambda g, b, c: (off_meta // N_META + b, g))
    return pl.pallas_call(
        _lru_prompt_kernel, grid=(D_RNN // cg, batch, nt),
        in_specs=[real, real, meta, meta] + _lru_param_specs(lambda g, b, c: g),
        out_specs=[real,
                   pl.BlockSpec((N_META, cg), lambda g, b, c: (b, g)),
                   pl.BlockSpec((None, 1, cg), lambda g, b, c: (b, 0, g))],
        out_shape=[jax.ShapeDtypeStruct((rows, D_RNN), MXU_DTYPE),
                   jax.ShapeDtypeStruct((batch * N_META, D_RNN), MXU_DTYPE),
                   jax.ShapeDtypeStruct((batch, 1, D_RNN), F32)],
        scratch_shapes=[pltpu.VMEM((V7X_SUBLANES, cg), F32)]
        + [pltpu.VMEM((V7X_SUBLANES + SEQ_TILE, cg), F32)] * 2,
        compiler_params=_cparams(3), name="rglru_prompt",
    )(xb, gate, xb, gate, *params)


def _lru_sample_kernel(x_ref, gate_ref, c0_ref, h0_ref, cw_ref, cb_ref, wa_ref, wx_ref, ba_ref, bx_ref,
                       cneg_ref, y_ref, hl_ref):
    ds = x_ref.shape[0]
    cw, cb = cw_ref[...], cb_ref[...]
    xp = [c0_ref[j] for j in range(CONV_W - 1)] + [x_ref[t] for t in range(ds)]
    xcs = []
    for t in range(ds):
        xc = cb + cw[0:1] * xp[t]
        for j in range(1, CONV_W):
            xc = xc + cw[j:j + 1] * xp[t + j]
        xcs.append(xc)
    xc = jnp.concatenate(xcs, axis=0)
    a, bb = _lru_gates(xc, wa_ref[...], wx_ref[...], ba_ref[...], bx_ref[...], cneg_ref[...])
    db = x_ref.shape[1]
    h = h0_ref[...]
    for t in range(ds):
        rs = slice(t * db, (t + 1) * db)
        h = a[rs] * h + bb[rs]
        y_ref[t] = (gate_ref[t] * h).astype(y_ref.dtype)
    hl_ref[...] = h


def _lru_sample(xb, gate, conv0, h0, params):
    ds, db, _ = xb.shape
    cg = LRU_GROUP
    tok = pl.BlockSpec((ds, db, cg), lambda g: (0, 0, g))
    return pl.pallas_call(
        _lru_sample_kernel, grid=(D_RNN // cg,),
        in_specs=[tok, tok,
                  pl.BlockSpec((CONV_W - 1, db, cg), lambda g: (0, 0, g)),
                  pl.BlockSpec((db, cg), lambda g: (0, g))] + _lru_param_specs(lambda g: g),
        out_specs=[tok, pl.BlockSpec((db, cg), lambda g: (0, g))],
        out_shape=[jax.ShapeDtypeStruct((ds, db, D_RNN), MXU_DTYPE),
                   jax.ShapeDtypeStruct((db, D_RNN), F32)],
        compiler_params=_cparams(1), name="rglru_sample",
    )(xb, gate, conv0, h0, *params)


def _pool_project(means, x, w_ref, scale, g, b):
    ys = []
    for gi in range(len(POOL_WINDOWS)):
        cs = slice(gi * POOL_GROUP, (gi + 1) * POOL_GROUP)
        ys.append(_mm(means[gi] - x[:, cs], w_ref[gi]))
    y = jnp.concatenate(ys, axis=1) * scale
    return _layer_norm_rows(ALPHA * x + y, g, b)


def _pool_tile(x, carry, pos0, w_ref, scale, g, b):
    n = x.shape[0]
    pg = POOL_GROUP

    def shift(v, prev8, k):
        if k < V7X_SUBLANES:
            return _shift_rows(v, prev8, k)
        if n == V7X_SUBLANES:
            return prev8
        return jnp.concatenate([prev8, v[:n - V7X_SUBLANES]], axis=0)

    levels = [x]
    for li, k in enumerate((1, 2, 4, 8)):
        v = levels[-1][:, pg:] if li > 0 else levels[-1]
        levels.append(v + shift(v, carry[li][:, D_MODEL - v.shape[1]:], k))
    pos = (pos0 + lax.broadcasted_iota(jnp.int32, (n, 1), 0)).astype(F32)
    means = []
    for gi, w in enumerate(POOL_WINDOWS):
        cnt = jnp.minimum(float(w), pos + 1.0)
        means.append(levels[gi + 1][:, :pg] / cnt)
    out = _pool_project(means, x, w_ref, scale, g, b)
    new_carry = []
    for li in range(4):
        v = levels[li]
        tail = v[n - V7X_SUBLANES:]
        if v.shape[1] < D_MODEL:
            tail = jnp.concatenate([jnp.zeros((V7X_SUBLANES, D_MODEL - v.shape[1]), F32), tail], axis=1)
        new_carry.append(tail)
    return out, new_carry


def _pool_prompt_kernel(x_ref, xm_ref, w_ref, sc_ref, g_ref, b_ref, o_ref, obf_ref, om_ref, ombf_ref,
                        carry_ref):
    c = pl.program_id(1)
    consts = (w_ref, sc_ref[...], g_ref[...], b_ref[...])

    @pl.when(c == 0)
    def _():
        zero = jnp.zeros((V7X_SUBLANES, D_MODEL), F32)
        out, carry = _pool_tile(xm_ref[...], [zero] * 4, 0, *consts)
        om_ref[...] = out
        ombf_ref[...] = out.astype(ombf_ref.dtype)
        for li in range(4):
            carry_ref[li] = carry[li]

    out, carry = _pool_tile(x_ref[...], [carry_ref[li] for li in range(4)],
                            N_META + c * x_ref.shape[0], *consts)
    o_ref[...] = out
    obf_ref[...] = out.astype(obf_ref.dtype)
    for li in range(4):
        carry_ref[li] = carry[li]


def _pool_prompt(x, w, scale, g, b, batch, seq, off_meta):
    rows = x.shape[0]
    nt = seq // SEQ_TILE
    ng = len(POOL_WINDOWS)
    assert seq % SEQ_TILE == 0 and off_meta % N_META == 0
    real = pl.BlockSpec((SEQ_TILE, D_MODEL), lambda bb, c: (bb * nt + c, 0))
    vec = pl.BlockSpec((1, D_MODEL), lambda bb, c: (0, 0))
    meta_out = pl.BlockSpec((N_META, D_MODEL), lambda bb, c: (bb, 0))
    return pl.pallas_call(
        _pool_prompt_kernel, grid=(batch, nt),
        in_specs=[real, pl.BlockSpec((N_META, D_MODEL), lambda bb, c: (off_meta // N_META + bb, 0)),
                  pl.BlockSpec((ng, POOL_GROUP, POOL_GROUP), lambda bb, c: (0, 0, 0)), vec, vec, vec],
        out_specs=[real, real, meta_out, meta_out],
        out_shape=[jax.ShapeDtypeStruct((rows, D_MODEL), F32), jax.ShapeDtypeStruct((rows, D_MODEL), MXU_DTYPE),
                   jax.ShapeDtypeStruct((batch * N_META, D_MODEL), F32),
                   jax.ShapeDtypeStruct((batch * N_META, D_MODEL), MXU_DTYPE)],
        scratch_shapes=[pltpu.VMEM((4, V7X_SUBLANES, D_MODEL), F32)],
        compiler_params=_cparams(2), name="pool_prompt",
    )(x, x, w, scale.reshape(1, D_MODEL), g.reshape(1, D_MODEL), b.reshape(1, D_MODEL))


def _pool_sample_kernel(x_ref, buf_ref, w_ref, sc_ref, g_ref, b_ref, o_ref, obf_ref, *, pos0):
    ds = x_ref.shape[0]
    nbuf = buf_ref.shape[0]
    pg = POOL_GROUP
    means = []
    for gi, w in enumerate(POOL_WINDOWS):
        cs = slice(gi * pg, (gi + 1) * pg)
        xp = [buf_ref[j, :, cs] for j in range(nbuf - (w - 1), nbuf)] + [x_ref[t, :, cs] for t in range(ds)]
        rows = []
        for t in range(ds):
            acc = xp[t]
            for j in range(1, w):
                acc = acc + xp[t + j]
            rows.append(acc / float(min(w, pos0 + t + 1)))
        means.append(jnp.concatenate(rows, axis=0))
    x = jnp.concatenate([x_ref[t] for t in range(ds)], axis=0)
    out = _pool_project(means, x, w_ref, sc_ref[...], g_ref[...], b_ref[...])
    bs = x_ref.shape[1]
    for t in range(ds):
        o_ref[t] = out[t * bs:(t + 1) * bs]
        obf_ref[t] = out[t * bs:(t + 1) * bs].astype(obf_ref.dtype)


def _pool_sample(x, buf, w, scale, g, b, pos0):
    ds, db, _ = x.shape
    nbuf = buf.shape[0]
    bs = min(POOL_SEQ_BLOCK, db)
    ng = len(POOL_WINDOWS)
    assert db % bs == 0
    tok = pl.BlockSpec((ds, bs, D_MODEL), lambda s: (0, s, 0))
    vec = pl.BlockSpec((1, D_MODEL), lambda s: (0, 0))
    return pl.pallas_call(
        functools.partial(_pool_sample_kernel, pos0=pos0), grid=(db // bs,),
        in_specs=[tok, pl.BlockSpec((nbuf, bs, D_MODEL), lambda s: (0, s, 0)),
                  pl.BlockSpec((ng, POOL_GROUP, POOL_GROUP), lambda s: (0, 0, 0)), vec, vec, vec],
        out_specs=[tok, tok],
        out_shape=[jax.ShapeDtypeStruct((ds, db, D_MODEL), F32), jax.ShapeDtypeStruct((ds, db, D_MODEL), MXU_DTYPE)],
        compiler_params=_cparams(1), name="pool_sample",
    )(x, buf, w, scale.reshape(1, D_MODEL), g.reshape(1, D_MODEL), b.reshape(1, D_MODEL))


def _block_diag_groups(w):
    per = LRU_GROUP // LRU_BLOCK
    w = w.reshape(LRU_BLOCKS // per, per, LRU_BLOCK, LRU_BLOCK)
    eye = jnp.eye(per, dtype=w.dtype)
    dense = jnp.einsum('gpcd,pq->gpcqd', w, eye)
    return dense.reshape(LRU_BLOCKS // per, LRU_GROUP, LRU_GROUP)


def kernel(x_prompt, x_sample, state_ret, state_lru_h, state_lru_conv, state_pool, meta_tokens,
           ret_w_in, ret_gn_g, ret_w_out, lru_w_in, lru_conv_w, lru_conv_b, lru_w_a, lru_b_a,
           lru_w_x, lru_b_x, lru_lambda, lru_w_out, pool_w, pool_scale, ln_mix_g, ln_mix_b,
           ln_ffn_g, ln_ffn_b, ffn_w1, ffn_w3, ffn_w2):
    batch, seq, d = x_prompt.shape
    db, ds, _ = x_sample.shape
    assert d == D_MODEL
    n_real, n_samp, n_meta = batch * seq, db * ds, batch * N_META
    off_s, off_m = n_real, n_real + n_samp
    mx = MXU_DTYPE

    x = jnp.concatenate([
        x_prompt.reshape(n_real, d),
        jnp.swapaxes(x_sample, 0, 1).reshape(n_samp, d),
        jnp.broadcast_to(meta_tokens[None].astype(x_prompt.dtype), (batch, N_META, d)).reshape(n_meta, d),
    ], axis=0)
    xbf = x.astype(mx)

    t_real = jnp.tile(jnp.arange(seq, dtype=F32), batch)
    t_samp = jnp.repeat(jnp.arange(ds, dtype=F32), db)
    t_meta = jnp.tile(jnp.arange(N_META, dtype=F32), batch)
    pos = jnp.concatenate([N_META + t_real, PAST_LEN + t_samp, t_meta])
    half = RET_DK // 2
    inv = ROPE_BASE ** (-jnp.arange(half, dtype=F32) / half)
    ang = pos[:, None] * inv[None, :]
    cos_t, sin_t = jnp.cos(ang), jnp.sin(ang)
    e_rows = jnp.concatenate([
        (RET_CHUNK - 1.0) - jnp.mod(t_real, float(RET_CHUNK)), (ds - 1.0) - t_samp, (N_META - 1.0) - t_meta,
    ])[:, None]
    lg_cols = jnp.repeat(jnp.asarray(_LOG_G, F32), RET_DK)[None, :]

    def sample_rows(a):
        return a[off_s:off_m].reshape(ds, db, a.shape[1])

    def merge(full, samp, meta):
        full = lax.dynamic_update_slice(full, samp.reshape(n_samp, samp.shape[-1]), (off_s, 0))
        return lax.dynamic_update_slice(full, meta, (off_m, 0))

    pad_rows = -(-ds // V7X_BF16_ROWS) * V7X_BF16_ROWS

    def to_seq_major(a):
        a = jnp.swapaxes(a, 0, 1)
        return jnp.pad(a, ((0, 0), (0, pad_rows - ds), (0, 0)))

    def last_rows(a, n):
        return jnp.stack([a[(b + 1) * seq - n:(b + 1) * seq] for b in range(batch)])

    ret_w_out_mx, lru_w_out_mx, w2_mx = ret_w_out.astype(mx), lru_w_out.astype(mx), ffn_w2.astype(mx)

    n_ret = ret_w_in.shape[0]
    ret_p = ret_s = None
    outs_h_p, outs_h_s, outs_conv_p, outs_conv_s, outs_pool_p, outs_pool_s = [], [], [], [], [], []

    for i in range(DEPTH):
        j, kind = i // 3, i % 3
        if kind == 0:
            (q,) = _project(_epi_q, xbf, [ret_w_in], j, [0], D_MODEL, 512, ROW_TILE, [mx],
                            row_inputs=[cos_t, sin_t], name="ret_q")
            k, kd = _project(_epi_k, xbf, [ret_w_in], j, [D_MODEL], D_MODEL, 512, ROW_TILE, [mx, mx],
                             row_inputs=[cos_t, sin_t, e_rows], col_inputs=[lg_cols], name="ret_k")
            (v,) = _project(_epi_cast, xbf, [ret_w_in], j, [2 * D_MODEL], D_MODEL, 512, ROW_TILE, [mx],
                            name="ret_v")
            (sg,) = _project(_epi_silu, xbf, [ret_w_in], j, [3 * D_MODEL], D_MODEL, 512, ROW_TILE, [F32],
                             name="ret_g")
            z, z_meta, ret_p = _retention_prompt(q, k, kd, v, sg, ret_gn_g[j], batch, seq, off_m, j, n_ret, ret_p)
            z_s, ret_s = _retention_sample(*(to_seq_major(sample_rows(a)) for a in (q, k, kd, v, sg)),
                                           ret_gn_g[j], state_ret, j, ds, ret_s)
            z = merge(z, jnp.swapaxes(z_s[:, :ds], 0, 1), z_meta)
            x, xbf = _residual_ln(z, ret_w_out_mx, j, x, ln_mix_g[i], ln_mix_b[i], 512, name="ret_out")
        elif kind == 1:
            (gate,) = _project(_epi_gelu, xbf, [lru_w_in], j, [0], D_RNN, 512, ROW_TILE, [F32], name="lru_gate")
            (xb,) = _project(_epi_cast, xbf, [lru_w_in], j, [D_RNN], D_RNN, 512, ROW_TILE, [F32], name="lru_x")
            params = (lru_conv_w[j], lru_conv_b[j].reshape(1, D_RNN),
                      _block_diag_groups(lru_w_a[j]).astype(mx), _block_diag_groups(lru_w_x[j]).astype(mx),
                      lru_b_a[j].reshape(1, D_RNN), lru_b_x[j].reshape(1, D_RNN),
                      (-LRU_C * jax.nn.softplus(-lru_lambda[j].astype(F32))).reshape(1, D_RNN))
            y, y_meta, h_p = _lru_prompt(xb, gate, params, batch, seq, off_m)
            xb_s = sample_rows(xb)
            y_s, h_s = _lru_sample(xb_s, sample_rows(gate), jnp.swapaxes(state_lru_conv[j], 0, 1).astype(F32),
                                   state_lru_h[j].astype(F32), params)
            y = merge(y, y_s, y_meta)
            x, xbf = _residual_ln(y, lru_w_out_mx, j, x, ln_mix_g[i], ln_mix_b[i], 512, name="lru_out")
            outs_h_p.append(h_p.reshape(batch, D_RNN))
            outs_h_s.append(h_s)
            assert seq >= CONV_W - 1
            outs_conv_p.append(last_rows(xb, CONV_W - 1))
            xp_s = jnp.concatenate([state_lru_conv[j].astype(F32), jnp.swapaxes(xb_s, 0, 1)], axis=1)
            outs_conv_s.append(xp_s[:, -(CONV_W - 1):])
        else:
            p = POOL_MAX - 1
            w_pool = pool_w[j].astype(mx)
            x_s = sample_rows(x)
            assert seq >= p
            outs_pool_p.append(last_rows(x, p))
            xp_s = jnp.concatenate([state_pool[j].astype(F32), jnp.swapaxes(x_s, 0, 1)], axis=1)
            outs_pool_s.append(xp_s[:, -p:])
            xo, xobf, xm, xmbf = _pool_prompt(x, w_pool, pool_scale[j], ln_mix_g[i], ln_mix_b[i], batch, seq, off_m)
            buf_tm = jnp.swapaxes(state_pool[j].astype(F32), 0, 1)
            xs, xsbf = _pool_sample(x_s, buf_tm, w_pool, pool_scale[j], ln_mix_g[i], ln_mix_b[i], PAST_LEN)
            x, xbf = merge(xo, xs, xm), merge(xobf, xsbf, xmbf)
        (hid,) = _project(_epi_swiglu, xbf, [ffn_w1, ffn_w3], i, [0, 0], D_FF, 512, FFN_ROW_TILE, [mx],
                          name="ffn_up")
        x, xbf = _residual_ln(hid, w2_mx, i, x, ln_ffn_g[i], ln_ffn_b[i], 512, name="ffn_down")

    y_prompt = x[:n_real].reshape(batch, seq, d)
    y_sample = jnp.swapaxes(x[off_s:off_m].reshape(ds, db, d), 0, 1)
    return (y_prompt, y_sample, ret_p, ret_s, jnp.stack(outs_h_p),
            jnp.stack(outs_h_s), jnp.stack(outs_conv_p), jnp.stack(outs_conv_s), jnp.stack(outs_pool_p),
            jnp.stack(outs_pool_s))
```

```python
import functools

import numpy as np
import jax
import jax.numpy as jnp
from jax import lax
from jax.experimental import pallas as pl
from jax.experimental.pallas import tpu as pltpu

F32 = jnp.float32
MXU_DTYPE = jnp.bfloat16

D_MODEL = 4096
N_META = 16
PAST_LEN = 16384
RET_HEADS = 16
RET_DK = D_MODEL // RET_HEADS
RET_CHUNK = 128
ROPE_BASE = 10000.0
D_RNN = 5632
LRU_BLOCKS = 16
LRU_BLOCK = D_RNN // LRU_BLOCKS
CONV_W = 4
LRU_C = 8.0
POOL_WINDOWS = (2, 4, 8, 16)
POOL_GROUP = D_MODEL // len(POOL_WINDOWS)
POOL_MAX = 16
D_FF = 11008
DEPTH = 4
ALPHA = (2.0 * DEPTH) ** 0.25
LN_EPS = 1e-5

V7X_SUBLANES = 8
V7X_LANES = 128
V7X_BF16_ROWS = 16
V7X_VMEM_LIMIT = 58 * 2 ** 20

ROW_TILE = 1104
FFN_ROW_TILE = 592
ACC_TILES_WIDE_K = (496, 1024, 2)
ACC_TILES_TALL = (800, 512, 1)
ACC_COL_CHUNK = 512
ACC_ROW_CHUNK = 16
SEQ_TILE = 256
RET_HEAD_GROUP = 16
RET_SAMPLE_BLOCK = 2
LRU_GROUP = 4 * LRU_BLOCK
POOL_SEQ_BLOCK = 32

_LOG_G = np.log1p(-np.exp2(-5.0 - np.arange(RET_HEADS, dtype=np.float64)))


def _cparams(n_axes):
    return pltpu.CompilerParams(dimension_semantics=("arbitrary",) * n_axes,
                                vmem_limit_bytes=V7X_VMEM_LIMIT)


def _mm(a, b):
    return jnp.dot(a.astype(MXU_DTYPE), b.astype(MXU_DTYPE), preferred_element_type=F32)


def _mm_nt(a, b):
    return lax.dot_general(a.astype(MXU_DTYPE), b.astype(MXU_DTYPE), (((1,), (1,)), ((), ())),
                           preferred_element_type=F32)


def _mm_tn(a, b):
    return lax.dot_general(a.astype(MXU_DTYPE), b.astype(MXU_DTYPE), (((0,), (0,)), ((), ())),
                           preferred_element_type=F32)


def _layer_norm_rows(y, g, b):
    mu = jnp.mean(y, axis=-1, keepdims=True)
    d = y - mu
    var = jnp.mean(d * d, axis=-1, keepdims=True)
    return d * lax.rsqrt(var + LN_EPS) * g + b


def _rope_cols(acc, cos, sin):
    half = RET_DK // 2
    parts = []
    for h in range(acc.shape[1] // RET_DK):
        x1 = acc[:, h * RET_DK: h * RET_DK + half]
        x2 = acc[:, h * RET_DK + half: (h + 1) * RET_DK]
        parts.append(x1 * cos - x2 * sin)
        parts.append(x1 * sin + x2 * cos)
    return jnp.concatenate(parts, axis=1)


def _epi_q(accs, cos_ref, sin_ref, o_ref):
    o_ref[...] = _rope_cols(accs[0], cos_ref[...], sin_ref[...]).astype(o_ref.dtype)


def _epi_k(accs, cos_ref, sin_ref, e_ref, lg_ref, k_ref, kd_ref):
    k = _rope_cols(accs[0], cos_ref[...], sin_ref[...]) * (RET_DK ** -0.5)
    k_ref[...] = k.astype(k_ref.dtype)
    kd_ref[...] = (k * jnp.exp(e_ref[...] * lg_ref[...])).astype(kd_ref.dtype)


def _epi_cast(accs, o_ref):
    o_ref[...] = accs[0].astype(o_ref.dtype)


def _epi_silu(accs, o_ref):
    o_ref[...] = (accs[0] * jax.nn.sigmoid(accs[0])).astype(o_ref.dtype)


def _epi_gelu(accs, o_ref):
    o_ref[...] = jax.nn.gelu(accs[0]).astype(o_ref.dtype)


def _epi_swiglu(accs, o_ref):
    a, b = accs
    o_ref[...] = (a * jax.nn.sigmoid(a) * b).astype(o_ref.dtype)


def _project_kernel(x_ref, *refs, epilogue, n_w, has_side):
    w_refs, mid, w_mx = refs[:n_w], refs[n_w:len(refs) - n_w], refs[len(refs) - n_w:]
    if has_side:
        side_in, side_out, mid = mid[0], mid[-1], mid[1:-1]
        side_out[...] = side_in[...].astype(side_out.dtype)

    @pl.when(pl.program_id(1) == 0)
    def _():
        for w_ref, s_ref in zip(w_refs, w_mx):
            s_ref[...] = w_ref[...].astype(s_ref.dtype)

    x = x_ref[...]
    epilogue([jnp.dot(x, s_ref[...], preferred_element_type=F32) for s_ref in w_mx], *mid)


def _project(epilogue, x, weights, layer, col_offsets, n_cols, bn, bm, out_dtypes, row_inputs=(),
             col_inputs=(), side=None, name=None):
    rows, k = x.shape
    bm = min(bm, -(-rows // V7X_BF16_ROWS) * V7X_BF16_ROWS)
    assert all(off % bn == 0 for off in col_offsets)
    grid = (pl.cdiv(n_cols, bn), pl.cdiv(rows, bm))
    in_specs = [pl.BlockSpec((bm, k), lambda j, i: (i, 0))]
    for off in col_offsets:
        in_specs.append(pl.BlockSpec((None, k, bn),
                                     functools.partial(lambda j, i, o: (layer, 0, j + o), o=off // bn)))
    out_shape = [jax.ShapeDtypeStruct((rows, n_cols), dt) for dt in out_dtypes]
    out_specs = [pl.BlockSpec((bm, bn), lambda j, i: (i, j)) for _ in out_dtypes]
    operands = [x, *weights]
    if side is not None:
        src, side_layer = side
        n_side, d_side = src.shape[1:]
        slab = -(-pl.cdiv(n_side, grid[0] * grid[1]) // V7X_BF16_ROWS) * V7X_BF16_ROWS
        last_slab = pl.cdiv(n_side, slab) - 1
        ni = grid[1]
        in_specs.append(pl.BlockSpec((None, slab, d_side),
                                     lambda j, i: (side_layer, jnp.minimum(j * ni + i, last_slab), 0)))
        out_specs.append(pl.BlockSpec((slab, d_side), lambda j, i: (jnp.minimum(j * ni + i, last_slab), 0)))
        out_shape.append(jax.ShapeDtypeStruct((n_side, d_side), MXU_DTYPE))
        operands.append(src)
    for r in row_inputs:
        in_specs.append(pl.BlockSpec((bm, r.shape[1]), lambda j, i: (i, 0)))
    for c in col_inputs:
        in_specs.append(pl.BlockSpec((1, bn), lambda j, i: (0, j)))
    return pl.pallas_call(
        functools.partial(_project_kernel, epilogue=epilogue, n_w=len(weights), has_side=side is not None),
        grid=grid, in_specs=in_specs, out_specs=out_specs, out_shape=out_shape,
        scratch_shapes=[pltpu.VMEM((k, bn), MXU_DTYPE) for _ in weights],
        compiler_params=_cparams(2), name=name,
    )(*operands, *row_inputs, *col_inputs)


def _residual_ln_kernel(a_ref, w_ref, x_hbm, g_ref, b_ref, o_ref, obf_ref, xbuf, sem, *, rows, kdim):
    i, k = pl.program_id(0), pl.program_id(1)
    n_tiles, n_k = pl.num_programs(0), pl.num_programs(1)
    bm, d = o_ref.shape
    tail = rows - (pl.cdiv(rows, bm) - 1) * bm

    def with_x_copy(fn):
        def copy(size):
            return pltpu.make_async_copy(
                x_hbm.at[pl.ds(pl.multiple_of(i * bm, V7X_BF16_ROWS), size), :],
                xbuf.at[pl.ds(0, size), :], sem)
        if tail == bm:
            fn(copy(bm))
        else:
            pl.when(i < n_tiles - 1)(lambda: fn(copy(bm)))
            pl.when(i == n_tiles - 1)(lambda: fn(copy(tail)))

    def accumulate(first, k_valid):
        a = a_ref[:, :k_valid]
        for n in range(d // ACC_COL_CHUNK):
            cs = slice(n * ACC_COL_CHUNK, (n + 1) * ACC_COL_CHUNK)
            p = _mm(a, w_ref[:k_valid, cs])
            o_ref[:, cs] = p if first else o_ref[:, cs] + p

    bk = a_ref.shape[1]
    k_tail = kdim - (pl.cdiv(kdim, bk) - 1) * bk

    @pl.when(k == 0)
    def _():
        with_x_copy(lambda cp: cp.start())
        accumulate(True, bk)

    @pl.when(jnp.logical_and(k > 0, k < n_k - 1))
    def _():
        accumulate(False, bk)

    @pl.when(jnp.logical_and(k > 0, k == n_k - 1))
    def _():
        accumulate(False, k_tail)

    @pl.when(k == n_k - 1)
    def _():
        with_x_copy(lambda cp: cp.wait())
        rc = ACC_ROW_CHUNK
        for r in range(bm // rc):
            rs = slice(r * rc, (r + 1) * rc)
            out = _layer_norm_rows(ALPHA * xbuf[rs, :] + o_ref[rs, :], g_ref[...], b_ref[...])
            o_ref[rs, :] = out
            obf_ref[rs, :] = out.astype(obf_ref.dtype)


def _residual_ln(a, w, layer, x, g, b, bm, bk, out_buffers, name=None):
    rows, kdim = a.shape
    d = w.shape[2]
    bm = min(bm, -(-rows // V7X_BF16_ROWS) * V7X_BF16_ROWS)
    n_k = pl.cdiv(kdim, bk)
    k_tail = kdim - (n_k - 1) * bk
    assert bm % ACC_ROW_CHUNK == 0 and (n_k > 1 or kdim == bk) and k_tail % V7X_LANES == 0
    n_tiles = pl.cdiv(rows, bm)
    assert n_tiles > 1 or rows == bm
    out_spec = pl.BlockSpec((bm, d), lambda i, k: (i, 0), pipeline_mode=pl.Buffered(out_buffers))
    return pl.pallas_call(
        functools.partial(_residual_ln_kernel, rows=rows, kdim=kdim), grid=(n_tiles, n_k),
        in_specs=[pl.BlockSpec((bm, bk), lambda i, k: (i, k)),
                  pl.BlockSpec((None, bk, d), lambda i, k: (layer, k, 0)),
                  pl.BlockSpec(memory_space=pl.ANY),
                  pl.BlockSpec((1, d), lambda i, k: (0, 0)),
                  pl.BlockSpec((1, d), lambda i, k: (0, 0))],
        out_specs=[out_spec, out_spec],
        out_shape=[jax.ShapeDtypeStruct((rows, d), F32), jax.ShapeDtypeStruct((rows, d), MXU_DTYPE)],
        scratch_shapes=[pltpu.VMEM((bm, d), F32), pltpu.SemaphoreType.DMA],
        compiler_params=_cparams(2), name=name,
    )(a, w, x, g.reshape(1, d), b.reshape(1, d))


def _group_norm_gate(o, sg, gn):
    mu = jnp.mean(o, axis=-1, keepdims=True)
    d = o - mu
    var = jnp.mean(d * d, axis=-1, keepdims=True)
    return sg * (d * lax.rsqrt(var + LN_EPS) * gn)


def _ret_prompt_kernel(q_ref, k_ref, kd_ref, v_ref, sg_ref,
                       qm_ref, km_ref, kdm_ref, vm_ref, sgm_ref,
                       gn_ref, dec_ref, qd_ref, cd_ref, z_ref, zm_ref, st_ref):
    c = pl.program_id(2)
    heads = dec_ref.shape[0]

    @pl.when(c == 0)
    def _():
        for h in range(heads):
            cs = slice(h * RET_DK, (h + 1) * RET_DK)
            q, k, kd, v = qm_ref[:, cs], km_ref[:, cs], kdm_ref[:, cs], vm_ref[:, cs]
            scores = _mm_nt(q, k) * dec_ref[h, :N_META, :N_META]
            o = _mm(scores, v)
            zm_ref[:, cs] = _group_norm_gate(o, sgm_ref[:, cs], gn_ref[:, cs]).astype(zm_ref.dtype)
            st_ref[0, h] = _mm_tn(kd, v)

    for h in range(heads):
        cs = slice(h * RET_DK, (h + 1) * RET_DK)
        q, k, kd, v = q_ref[:, cs], k_ref[:, cs], kd_ref[:, cs], v_ref[:, cs]
        s = st_ref[0, h]
        scores = _mm_nt(q, k) * dec_ref[h]
        o = _mm(scores, v) + _mm(q, s) * qd_ref[h]
        z_ref[:, cs] = _group_norm_gate(o, sg_ref[:, cs], gn_ref[:, cs]).astype(z_ref.dtype)
        st_ref[0, h] = cd_ref[h] * s + _mm_tn(kd, v)


def _retention_tables(length):
    idx = np.arange(length, dtype=np.float64)
    diff = idx[:, None] - idx[None, :]
    dec = np.where(diff >= 0, np.exp(_LOG_G[:, None, None] * np.maximum(diff, 0.0)), 0.0)
    qd = np.exp(_LOG_G[:, None] * (idx[None, :] + 1.0))
    return dec.astype(np.float32), qd.astype(np.float32)


def _drop_ref(body, index):
    def wrapped(*refs):
        body(*refs[:index], *refs[index + 1:])
    return wrapped


def _retention_prompt(q, k, kd, v, sg, gn, batch, seq, off_meta, layer, n_layers, states):
    rows = q.shape[0]
    nc = seq // RET_CHUNK
    hg = RET_HEAD_GROUP
    wcols = hg * RET_DK
    dec, qd = _retention_tables(RET_CHUNK)
    qd_full = np.broadcast_to(qd[:, :, None], (RET_HEADS, RET_CHUNK, RET_DK)).copy()
    cd = np.broadcast_to(np.exp(_LOG_G * RET_CHUNK)[:, None, None], (RET_HEADS, 1, RET_DK)).astype(np.float32)
    assert off_meta % N_META == 0 and seq % RET_CHUNK == 0
    real = pl.BlockSpec((RET_CHUNK, wcols), lambda b, g, c: (b * nc + c, g))
    meta = pl.BlockSpec((N_META, wcols), lambda b, g, c: (off_meta // N_META + b, g))
    inputs = [q, k, kd, v, sg, q, k, kd, v, sg, gn.reshape(1, D_MODEL), jnp.asarray(dec), jnp.asarray(qd_full),
              jnp.asarray(cd)]
    in_specs = [real] * 5 + [meta] * 5 + [
        pl.BlockSpec((1, wcols), lambda b, g, c: (0, g)),
        pl.BlockSpec((hg, RET_CHUNK, RET_CHUNK), lambda b, g, c: (g, 0, 0)),
        pl.BlockSpec((hg, RET_CHUNK, RET_DK), lambda b, g, c: (g, 0, 0)),
        pl.BlockSpec((hg, 1, RET_DK), lambda b, g, c: (g, 0, 0))]
    body, aliases = _ret_prompt_kernel, {}
    if states is not None:
        body, aliases = _drop_ref(body, len(inputs)), {len(inputs): 2}
        inputs.append(states)
        in_specs.append(pl.BlockSpec(memory_space=pl.ANY))
    return pl.pallas_call(
        body, grid=(batch, RET_HEADS // hg, nc), in_specs=in_specs,
        out_specs=[real,
                   pl.BlockSpec((N_META, wcols), lambda b, g, c: (b, g)),
                   pl.BlockSpec((None, 1, hg, RET_DK, RET_DK), lambda b, g, c: (layer, b, g, 0, 0))],
        out_shape=[jax.ShapeDtypeStruct((rows, D_MODEL), MXU_DTYPE),
                   jax.ShapeDtypeStruct((batch * N_META, D_MODEL), MXU_DTYPE),
                   jax.ShapeDtypeStruct((n_layers, batch, RET_HEADS, RET_DK, RET_DK), F32)],
        input_output_aliases=aliases, compiler_params=_cparams(3), name="retention_prompt",
    )(*inputs)


def _ret_sample_kernel(q_ref, k_ref, kd_ref, v_ref, sg_ref, s0_ref, gn_ref, dec_ref, qd_ref, cd_ref,
                       z_ref, st_ref):
    for b in range(q_ref.shape[0]):
        for h in range(RET_HEADS):
            cs = slice(h * RET_DK, (h + 1) * RET_DK)
            q, k, kd, v = q_ref[b, :, cs], k_ref[b, :, cs], kd_ref[b, :, cs], v_ref[b, :, cs]
            s = s0_ref[b, h]
            scores = _mm_nt(q, k) * dec_ref[h]
            o = _mm(scores, v) + _mm(q, s) * qd_ref[h]
            z_ref[b, :, cs] = _group_norm_gate(o, sg_ref[b, :, cs], gn_ref[:, cs]).astype(z_ref.dtype)
            st_ref[b, h] = cd_ref[h] * s + _mm_tn(kd, v)


def _retention_sample(q, k, kd, v, sg, gn, state, layer, dec_seq, states):
    db, pad_rows, _ = q.shape
    dec, qd = _retention_tables(dec_seq)
    dec_p = np.zeros((RET_HEADS, pad_rows, pad_rows), np.float32)
    dec_p[:, :dec_seq, :dec_seq] = dec
    qd_p = np.zeros((RET_HEADS, pad_rows, RET_DK), np.float32)
    qd_p[:, :dec_seq, :] = qd[:, :, None]
    cd = np.broadcast_to(np.exp(_LOG_G * dec_seq)[:, None, None], (RET_HEADS, 1, RET_DK)).astype(np.float32)
    sb = RET_SAMPLE_BLOCK if db % RET_SAMPLE_BLOCK == 0 else 1
    tok = pl.BlockSpec((sb, pad_rows, D_MODEL), lambda b: (b, 0, 0))
    st = pl.BlockSpec((None, sb, RET_HEADS, RET_DK, RET_DK), lambda b: (layer, b, 0, 0, 0))
    inputs = [q, k, kd, v, sg, state, gn.reshape(1, D_MODEL), jnp.asarray(dec_p), jnp.asarray(qd_p), jnp.asarray(cd)]
    in_specs = [tok] * 5 + [
        st, pl.BlockSpec((1, D_MODEL), lambda b: (0, 0)),
        pl.BlockSpec((RET_HEADS, pad_rows, pad_rows), lambda b: (0, 0, 0)),
        pl.BlockSpec((RET_HEADS, pad_rows, RET_DK), lambda b: (0, 0, 0)),
        pl.BlockSpec((RET_HEADS, 1, RET_DK), lambda b: (0, 0, 0))]
    body, aliases = _ret_sample_kernel, {}
    if states is not None:
        body, aliases = _drop_ref(body, len(inputs)), {len(inputs): 1}
        inputs.append(states)
        in_specs.append(pl.BlockSpec(memory_space=pl.ANY))
    return pl.pallas_call(
        body, grid=(db // sb,), in_specs=in_specs, out_specs=[tok, st],
        out_shape=[jax.ShapeDtypeStruct((db, pad_rows, D_MODEL), MXU_DTYPE),
                   jax.ShapeDtypeStruct(state.shape, F32)],
        input_output_aliases=aliases, compiler_params=_cparams(1), name="retention_sample",
    )(*inputs)


def _shift_rows(x, prev8, j):
    xr = pltpu.roll(x, j, 0)
    pr = pltpu.roll(prev8, j, 0)
    row = lax.broadcasted_iota(jnp.int32, prev8.shape, 0)
    first = jnp.where(row < j, pr, xr[:V7X_SUBLANES])
    if x.shape[0] == V7X_SUBLANES:
        return first
    return jnp.concatenate([first, xr[V7X_SUBLANES:]], axis=0)


def _lru_gates(xc, wa, wx, ba, bx, cneg):
    r = jax.nn.sigmoid(_mm(xc, wa) + ba)
    i = jax.nn.sigmoid(_mm(xc, wx) + bx)
    log_a = cneg * r
    a = jnp.exp(log_a)
    z = 1.0 - a * a
    return a, jnp.where(z > 0.0, z * lax.rsqrt(z), 0.0) * (i * xc)


def _lru_tile(x, gate, hprev, xs_ref, sb_ref, cw, cb, wa, wx, ba, bx, cneg):
    n = x.shape[0]
    o = V7X_SUBLANES
    xs_ref[o:o + n, :] = x
    xc = cb + cw[3:4] * x
    for j in range(1, CONV_W):
        xc = xc + cw[3 - j:4 - j] * xs_ref[o - j:o - j + n, :]
    xs_ref[0:o, :] = x[n - o:]
    a, bb = _lru_gates(xc, wa, wx, ba, bx, cneg)
    groups = n // V7X_SUBLANES
    a = a.reshape(groups, V7X_SUBLANES, a.shape[1])
    bb = bb.reshape(groups, V7X_SUBLANES, bb.shape[1])
    row = lax.broadcasted_iota(jnp.int32, (1,) + a.shape[1:], 1)
    for s in (1, 2, 4):
        m = row >= s
        a_s = jnp.where(m, pltpu.roll(a, s, 1), 1.0)
        b_s = jnp.where(m, pltpu.roll(bb, s, 1), 0.0)
        bb = bb + a * b_s
        a = a * a_s
    h = hprev
    for g in range(groups):
        hg = bb[g] + a[g] * h
        sb_ref[o + g * V7X_SUBLANES:o + (g + 1) * V7X_SUBLANES, :] = hg
        h = hg[V7X_SUBLANES - 1:V7X_SUBLANES]
    return gate * sb_ref[o:o + n, :], h


def _lru_prompt_kernel(x_ref, gate_ref, xm_ref, gatem_ref, cw_ref, cb_ref, wa_ref, wx_ref, ba_ref, bx_ref,
                       cneg_ref, y_ref, ym_ref, hl_ref, h_ref, xs_ref, sb_ref):
    c = pl.program_id(2)
    scratch = (xs_ref, sb_ref)
    params = (cw_ref[...], cb_ref[...], wa_ref[...], wx_ref[...], ba_ref[...], bx_ref[...], cneg_ref[...])

    @pl.when(c == 0)
    def _():
        zeros8 = jnp.zeros(h_ref.shape, F32)
        for ref in scratch:
            ref[0:V7X_SUBLANES, :] = zeros8
        y, h = _lru_tile(xm_ref[...], gatem_ref[...], zeros8[0:1], *scratch, *params)
        ym_ref[...] = y.astype(ym_ref.dtype)
        h_ref[...] = jnp.broadcast_to(h, h_ref.shape)

    y, h = _lru_tile(x_ref[...], gate_ref[...], h_ref[0:1, :], *scratch, *params)
    y_ref[...] = y.astype(y_ref.dtype)
    h_ref[...] = jnp.broadcast_to(h, h_ref.shape)
    hl_ref[...] = h


def _lru_param_specs(imap):
    cg = LRU_GROUP
    return [pl.BlockSpec((CONV_W, cg), lambda *a: (0, imap(*a))),
            pl.BlockSpec((1, cg), lambda *a: (0, imap(*a))),
            pl.BlockSpec((None, cg, cg), lambda *a: (imap(*a), 0, 0)),
            pl.BlockSpec((None, cg, cg), lambda *a: (imap(*a), 0, 0)),
            pl.BlockSpec((1, cg), lambda *a: (0, imap(*a))),
            pl.BlockSpec((1, cg), lambda *a: (0, imap(*a))),
            pl.BlockSpec((1, cg), lambda *a: (0, imap(*a)))]


def _lru_prompt(xb, gate, params, batch, seq, off_meta):
    rows = xb.shape[0]
    cg = LRU_GROUP
    nt = seq // SEQ_TILE
    assert seq % SEQ_TILE == 0 and off_meta % N_META == 0
    real = pl.BlockSpec((SEQ_TILE, cg), lambda g, b, c: (b * nt + c, g))
    meta = pl.BlockSpec((N_META, cg), lambda g, b, c: (off_meta // N_META + b, g))
    return pl.pallas_call(
        _lru_prompt_kernel, grid=(D_RNN // cg, batch, nt),
        in_specs=[real, real, meta, meta] + _lru_param_specs(lambda g, b, c: g),
        out_specs=[real,
                   pl.BlockSpec((N_META, cg), lambda g, b, c: (b, g)),
                   pl.BlockSpec((None, 1, cg), lambda g, b, c: (b, 0, g))],
        out_shape=[jax.ShapeDtypeStruct((rows, D_RNN), MXU_DTYPE),
                   jax.ShapeDtypeStruct((batch * N_META, D_RNN), MXU_DTYPE),
                   jax.ShapeDtypeStruct((batch, 1, D_RNN), F32)],
        scratch_shapes=[pltpu.VMEM((V7X_SUBLANES, cg), F32)]
        + [pltpu.VMEM((V7X_SUBLANES + SEQ_TILE, cg), F32)] * 2,
        compiler_params=_cparams(3), name="rglru_prompt",
    )(xb, gate, xb, gate, *params)


def _lru_sample_kernel(x_ref, gate_ref, c0_ref, h0_ref, cw_ref, cb_ref, wa_ref, wx_ref, ba_ref, bx_ref,
                       cneg_ref, y_ref, hl_ref):
    ds = x_ref.shape[0]
    cw, cb = cw_ref[...], cb_ref[...]
    xp = [c0_ref[j] for j in range(CONV_W - 1)] + [x_ref[t] for t in range(ds)]
    xcs = []
    for t in range(ds):
        xc = cb + cw[0:1] * xp[t]
        for j in range(1, CONV_W):
            xc = xc + cw[j:j + 1] * xp[t + j]
        xcs.append(xc)
    xc = jnp.concatenate(xcs, axis=0)
    a, bb = _lru_gates(xc, wa_ref[...], wx_ref[...], ba_ref[...], bx_ref[...], cneg_ref[...])
    db = x_ref.shape[1]
    h = h0_ref[...]
    for t in range(ds):
        rs = slice(t * db, (t + 1) * db)
        h = a[rs] * h + bb[rs]
        y_ref[t] = (gate_ref[t] * h).astype(y_ref.dtype)
    hl_ref[...] = h


def _lru_sample(xb, gate, conv0, h0, params):
    ds, db, _ = xb.shape
    cg = LRU_GROUP
    tok = pl.BlockSpec((ds, db, cg), lambda g: (0, 0, g))
    return pl.pallas_call(
        _lru_sample_kernel, grid=(D_RNN // cg,),
        in_specs=[tok, tok,
                  pl.BlockSpec((CONV_W - 1, db, cg), lambda g: (0, 0, g)),
                  pl.BlockSpec((db, cg), lambda g: (0, g))] + _lru_param_specs(lambda g: g),
        out_specs=[tok, pl.BlockSpec((db, cg), lambda g: (0, g))],
        out_shape=[jax.ShapeDtypeStruct((ds, db, D_RNN), MXU_DTYPE),
                   jax.ShapeDtypeStruct((db, D_RNN), F32)],
        compiler_params=_cparams(1), name="rglru_sample",
    )(xb, gate, conv0, h0, *params)


def _pool_project(means, x, w_ref, scale, g, b):
    ys = []
    for gi in range(len(POOL_WINDOWS)):
        cs = slice(gi * POOL_GROUP, (gi + 1) * POOL_GROUP)
        ys.append(_mm(means[gi] - x[:, cs], w_ref[gi]))
    y = jnp.concatenate(ys, axis=1) * scale
    return _layer_norm_rows(ALPHA * x + y, g, b)


def _pool_tile(x, carry, pos0, w_ref, scale, g, b):
    n = x.shape[0]
    pg = POOL_GROUP

    def shift(v, prev8, k):
        if k < V7X_SUBLANES:
            return _shift_rows(v, prev8, k)
        if n == V7X_SUBLANES:
            return prev8
        return jnp.concatenate([prev8, v[:n - V7X_SUBLANES]], axis=0)

    levels = [x]
    for li, k in enumerate((1, 2, 4, 8)):
        v = levels[-1][:, pg:] if li > 0 else levels[-1]
        levels.append(v + shift(v, carry[li][:, D_MODEL - v.shape[1]:], k))
    pos = (pos0 + lax.broadcasted_iota(jnp.int32, (n, 1), 0)).astype(F32)
    means = []
    for gi, w in enumerate(POOL_WINDOWS):
        cnt = jnp.minimum(float(w), pos + 1.0)
        means.append(levels[gi + 1][:, :pg] / cnt)
    out = _pool_project(means, x, w_ref, scale, g, b)
    new_carry = []
    for li in range(4):
        v = levels[li]
        tail = v[n - V7X_SUBLANES:]
        if v.shape[1] < D_MODEL:
            tail = jnp.concatenate([jnp.zeros((V7X_SUBLANES, D_MODEL - v.shape[1]), F32), tail], axis=1)
        new_carry.append(tail)
    return out, new_carry


def _pool_prompt_kernel(x_ref, xm_ref, w_ref, sc_ref, g_ref, b_ref, o_ref, obf_ref, om_ref, ombf_ref,
                        carry_ref):
    c = pl.program_id(1)
    consts = (w_ref, sc_ref[...], g_ref[...], b_ref[...])

    @pl.when(c == 0)
    def _():
        zero = jnp.zeros((V7X_SUBLANES, D_MODEL), F32)
        out, carry = _pool_tile(xm_ref[...], [zero] * 4, 0, *consts)
        om_ref[...] = out
        ombf_ref[...] = out.astype(ombf_ref.dtype)
        for li in range(4):
            carry_ref[li] = carry[li]

    out, carry = _pool_tile(x_ref[...], [carry_ref[li] for li in range(4)],
                            N_META + c * x_ref.shape[0], *consts)
    o_ref[...] = out
    obf_ref[...] = out.astype(obf_ref.dtype)
    for li in range(4):
        carry_ref[li] = carry[li]


def _pool_prompt(x, w, scale, g, b, batch, seq, off_meta):
    rows = x.shape[0]
    nt = seq // SEQ_TILE
    ng = len(POOL_WINDOWS)
    assert seq % SEQ_TILE == 0 and off_meta % N_META == 0
    real = pl.BlockSpec((SEQ_TILE, D_MODEL), lambda bb, c: (bb * nt + c, 0))
    vec = pl.BlockSpec((1, D_MODEL), lambda bb, c: (0, 0))
    meta_out = pl.BlockSpec((N_META, D_MODEL), lambda bb, c: (bb, 0))
    return pl.pallas_call(
        _pool_prompt_kernel, grid=(batch, nt),
        in_specs=[real, pl.BlockSpec((N_META, D_MODEL), lambda bb, c: (off_meta // N_META + bb, 0)),
                  pl.BlockSpec((ng, POOL_GROUP, POOL_GROUP), lambda bb, c: (0, 0, 0)), vec, vec, vec],
        out_specs=[real, real, meta_out, meta_out],
        out_shape=[jax.ShapeDtypeStruct((rows, D_MODEL), F32), jax.ShapeDtypeStruct((rows, D_MODEL), MXU_DTYPE),
                   jax.ShapeDtypeStruct((batch * N_META, D_MODEL), F32),
                   jax.ShapeDtypeStruct((batch * N_META, D_MODEL), MXU_DTYPE)],
        scratch_shapes=[pltpu.VMEM((4, V7X_SUBLANES, D_MODEL), F32)],
        compiler_params=_cparams(2), name="pool_prompt",
    )(x, x, w, scale.reshape(1, D_MODEL), g.reshape(1, D_MODEL), b.reshape(1, D_MODEL))


def _pool_sample_kernel(x_ref, buf_ref, w_ref, sc_ref, g_ref, b_ref, o_ref, obf_ref, *, pos0):
    ds = x_ref.shape[0]
    nbuf = buf_ref.shape[0]
    pg = POOL_GROUP
    means = []
    for gi, w in enumerate(POOL_WINDOWS):
        cs = slice(gi * pg, (gi + 1) * pg)
        xp = [buf_ref[j, :, cs] for j in range(nbuf - (w - 1), nbuf)] + [x_ref[t, :, cs] for t in range(ds)]
        rows = []
        for t in range(ds):
            acc = xp[t]
            for j in range(1, w):
                acc = acc + xp[t + j]
            rows.append(acc / float(min(w, pos0 + t + 1)))
        means.append(jnp.concatenate(rows, axis=0))
    x = jnp.concatenate([x_ref[t] for t in range(ds)], axis=0)
    out = _pool_project(means, x, w_ref, sc_ref[...], g_ref[...], b_ref[...])
    bs = x_ref.shape[1]
    for t in range(ds):
        o_ref[t] = out[t * bs:(t + 1) * bs]
        obf_ref[t] = out[t * bs:(t + 1) * bs].astype(obf_ref.dtype)


def _pool_sample(x, buf, w, scale, g, b, pos0):
    ds, db, _ = x.shape
    nbuf = buf.shape[0]
    bs = min(POOL_SEQ_BLOCK, db)
    ng = len(POOL_WINDOWS)
    assert db % bs == 0
    tok = pl.BlockSpec((ds, bs, D_MODEL), lambda s: (0, s, 0))
    vec = pl.BlockSpec((1, D_MODEL), lambda s: (0, 0))
    return pl.pallas_call(
        functools.partial(_pool_sample_kernel, pos0=pos0), grid=(db // bs,),
        in_specs=[tok, pl.BlockSpec((nbuf, bs, D_MODEL), lambda s: (0, s, 0)),
                  pl.BlockSpec((ng, POOL_GROUP, POOL_GROUP), lambda s: (0, 0, 0)), vec, vec, vec],
        out_specs=[tok, tok],
        out_shape=[jax.ShapeDtypeStruct((ds, db, D_MODEL), F32), jax.ShapeDtypeStruct((ds, db, D_MODEL), MXU_DTYPE)],
        compiler_params=_cparams(1), name="pool_sample",
    )(x, buf, w, scale.reshape(1, D_MODEL), g.reshape(1, D_MODEL), b.reshape(1, D_MODEL))


def _block_diag_groups(w):
    per = LRU_GROUP // LRU_BLOCK
    w = w.reshape(LRU_BLOCKS // per, per, LRU_BLOCK, LRU_BLOCK)
    eye = jnp.eye(per, dtype=w.dtype)
    dense = jnp.einsum('gpcd,pq->gpcqd', w, eye)
    return dense.reshape(LRU_BLOCKS // per, LRU_GROUP, LRU_GROUP)


def kernel(x_prompt, x_sample, state_ret, state_lru_h, state_lru_conv, state_pool, meta_tokens,
           ret_w_in, ret_gn_g, ret_w_out, lru_w_in, lru_conv_w, lru_conv_b, lru_w_a, lru_b_a,
           lru_w_x, lru_b_x, lru_lambda, lru_w_out, pool_w, pool_scale, ln_mix_g, ln_mix_b,
           ln_ffn_g, ln_ffn_b, ffn_w1, ffn_w3, ffn_w2):
    batch, seq, d = x_prompt.shape
    db, ds, _ = x_sample.shape
    assert d == D_MODEL
    n_real, n_samp, n_meta = batch * seq, db * ds, batch * N_META
    off_s, off_m = n_real, n_real + n_samp
    mx = MXU_DTYPE

    x = jnp.concatenate([
        x_prompt.reshape(n_real, d),
        jnp.swapaxes(x_sample, 0, 1).reshape(n_samp, d),
        jnp.broadcast_to(meta_tokens[None].astype(x_prompt.dtype), (batch, N_META, d)).reshape(n_meta, d),
    ], axis=0)
    xbf = x.astype(mx)

    t_real = jnp.tile(jnp.arange(seq, dtype=F32), batch)
    t_samp = jnp.repeat(jnp.arange(ds, dtype=F32), db)
    t_meta = jnp.tile(jnp.arange(N_META, dtype=F32), batch)
    pos = jnp.concatenate([N_META + t_real, PAST_LEN + t_samp, t_meta])
    half = RET_DK // 2
    inv = ROPE_BASE ** (-jnp.arange(half, dtype=F32) / half)
    ang = pos[:, None] * inv[None, :]
    cos_t, sin_t = jnp.cos(ang), jnp.sin(ang)
    e_rows = jnp.concatenate([
        (RET_CHUNK - 1.0) - jnp.mod(t_real, float(RET_CHUNK)), (ds - 1.0) - t_samp, (N_META - 1.0) - t_meta,
    ])[:, None]
    lg_cols = jnp.repeat(jnp.asarray(_LOG_G, F32), RET_DK)[None, :]

    def sample_rows(a):
        return a[off_s:off_m].reshape(ds, db, a.shape[1])

    def merge(full, samp, meta):
        full = lax.dynamic_update_slice(full, samp.reshape(n_samp, samp.shape[-1]), (off_s, 0))
        return lax.dynamic_update_slice(full, meta, (off_m, 0))

    pad_rows = -(-ds // V7X_BF16_ROWS) * V7X_BF16_ROWS

    def to_seq_major(a):
        a = jnp.swapaxes(a, 0, 1)
        return jnp.pad(a, ((0, 0), (0, pad_rows - ds), (0, 0)))

    def last_rows(a, n):
        return jnp.stack([a[(b + 1) * seq - n:(b + 1) * seq] for b in range(batch)])

    def stacked_rows(w):
        return w.reshape(1, w.shape[0] * w.shape[1], w.shape[2])

    w2_mx = {}

    n_ret = ret_w_in.shape[0]
    ret_p = ret_s = None
    outs_h_p, outs_h_s, outs_conv_p, outs_conv_s, outs_pool_p, outs_pool_s = [], [], [], [], [], []

    for i in range(DEPTH):
        j, kind = i // 3, i % 3
        if kind == 0:
            first = j == 0
            q, w2_mx[i] = _project(_epi_q, xbf, [ret_w_in], j, [0], D_MODEL, 512, ROW_TILE, [mx],
                                   row_inputs=[cos_t, sin_t], side=(ffn_w2, i), name="ret_q")
            k, kd, *side_k = _project(_epi_k, xbf, [ret_w_in], j, [D_MODEL], D_MODEL, 512, ROW_TILE, [mx, mx],
                                      row_inputs=[cos_t, sin_t, e_rows], col_inputs=[lg_cols],
                                      side=(stacked_rows(lru_w_out), 0) if first else None, name="ret_k")
            v, *side_v = _project(_epi_cast, xbf, [ret_w_in], j, [2 * D_MODEL], D_MODEL, 512, ROW_TILE, [mx],
                                  side=(ffn_w2, i + 2) if first and i + 2 < DEPTH else None, name="ret_v")
            sg, *side_g = _project(_epi_silu, xbf, [ret_w_in], j, [3 * D_MODEL], D_MODEL, 512, ROW_TILE, [F32],
                                   side=(stacked_rows(ret_w_out), 0) if first else None, name="ret_g")
            if first:
                lru_w_out_mx = side_k[0].reshape(lru_w_out.shape)
                ret_w_out_mx = side_g[0].reshape(ret_w_out.shape)
                if side_v:
                    w2_mx[i + 2] = side_v[0]
            z, z_meta, ret_p = _retention_prompt(q, k, kd, v, sg, ret_gn_g[j], batch, seq, off_m, j, n_ret, ret_p)
            z_s, ret_s = _retention_sample(*(to_seq_major(sample_rows(a)) for a in (q, k, kd, v, sg)),
                                           ret_gn_g[j], state_ret, j, ds, ret_s)
            z = merge(z, jnp.swapaxes(z_s[:, :ds], 0, 1), z_meta)
            x, xbf = _residual_ln(z, ret_w_out_mx, j, x, ln_mix_g[i], ln_mix_b[i], *ACC_TILES_WIDE_K, name="ret_out")
        elif kind == 1:
            gate, w2_mx[i] = _project(_epi_gelu, xbf, [lru_w_in], j, [0], D_RNN, 512, ROW_TILE, [F32],
                                      side=(ffn_w2, i), name="lru_gate")
            (xb,) = _project(_epi_cast, xbf, [lru_w_in], j, [D_RNN], D_RNN, 512, ROW_TILE, [F32], name="lru_x")
            params = (lru_conv_w[j], lru_conv_b[j].reshape(1, D_RNN),
                      _block_diag_groups(lru_w_a[j]).astype(mx), _block_diag_groups(lru_w_x[j]).astype(mx),
                      lru_b_a[j].reshape(1, D_RNN), lru_b_x[j].reshape(1, D_RNN),
                      (-LRU_C * jax.nn.softplus(-lru_lambda[j].astype(F32))).reshape(1, D_RNN))
            y, y_meta, h_p = _lru_prompt(xb, gate, params, batch, seq, off_m)
            xb_s = sample_rows(xb)
            y_s, h_s = _lru_sample(xb_s, sample_rows(gate), jnp.swapaxes(state_lru_conv[j], 0, 1).astype(F32),
                                   state_lru_h[j].astype(F32), params)
            y = merge(y, y_s, y_meta)
            x, xbf = _residual_ln(y, lru_w_out_mx, j, x, ln_mix_g[i], ln_mix_b[i], *ACC_TILES_TALL, name="lru_out")
            outs_h_p.append(h_p.reshape(batch, D_RNN))
            outs_h_s.append(h_s)
            assert seq >= CONV_W - 1
            outs_conv_p.append(last_rows(xb, CONV_W - 1))
            xp_s = jnp.concatenate([state_lru_conv[j].astype(F32), jnp.swapaxes(xb_s, 0, 1)], axis=1)
            outs_conv_s.append(xp_s[:, -(CONV_W - 1):])
        else:
            p = POOL_MAX - 1
            w_pool = pool_w[j].astype(mx)
            x_s = sample_rows(x)
            assert seq >= p
            outs_pool_p.append(last_rows(x, p))
            xp_s = jnp.concatenate([state_pool[j].astype(F32), jnp.swapaxes(x_s, 0, 1)], axis=1)
            outs_pool_s.append(xp_s[:, -p:])
            xo, xobf, xm, xmbf = _pool_prompt(x, w_pool, pool_scale[j], ln_mix_g[i], ln_mix_b[i], batch, seq, off_m)
            buf_tm = jnp.swapaxes(state_pool[j].astype(F32), 0, 1)
            xs, xsbf = _pool_sample(x_s, buf_tm, w_pool, pool_scale[j], ln_mix_g[i], ln_mix_b[i], PAST_LEN)
            x, xbf = merge(xo, xs, xm), merge(xobf, xsbf, xmbf)
        (hid,) = _project(_epi_swiglu, xbf, [ffn_w1, ffn_w3], i, [0, 0], D_FF, 512, FFN_ROW_TILE, [mx],
                          name="ffn_up")
        x, xbf = _residual_ln(hid, w2_mx[i][None], 0, x, ln_ffn_g[i], ln_ffn_b[i], *ACC_TILES_WIDE_K,
                              name="ffn_down")

    y_prompt = x[:n_real].reshape(batch, seq, d)
    y_sample = jnp.swapaxes(x[off_s:off_m].reshape(ds, db, d), 0, 1)
    return (y_prompt, y_sample, ret_p, ret_s, jnp.stack(outs_h_p),
            jnp.stack(outs_h_s), jnp.stack(outs_conv_p), jnp.stack(outs_conv_s), jnp.stack(outs_pool_p),
            jnp.stack(outs_pool_s))
```

```python
import functools

import numpy as np
import jax
import jax.numpy as jnp
from jax import lax
from jax.experimental import pallas as pl
from jax.experimental.pallas import tpu as pltpu

F32 = jnp.float32
MXU_DTYPE = jnp.bfloat16

D_MODEL = 4096
N_META = 16
PAST_LEN = 16384
RET_HEADS = 16
RET_DK = D_MODEL // RET_HEADS
RET_CHUNK = 128
ROPE_BASE = 10000.0
D_RNN = 5632
LRU_BLOCKS = 16
LRU_BLOCK = D_RNN // LRU_BLOCKS
CONV_W = 4
LRU_C = 8.0
POOL_WINDOWS = (2, 4, 8, 16)
POOL_GROUP = D_MODEL // len(POOL_WINDOWS)
POOL_MAX = 16
D_FF = 11008
DEPTH = 4
ALPHA = (2.0 * DEPTH) ** 0.25
LN_EPS = 1e-5

V7X_SUBLANES = 8
V7X_LANES = 128
V7X_BF16_ROWS = 16
V7X_VMEM_LIMIT = 58 * 2 ** 20

ROW_TILE = 1104
FFN_ROW_TILE = 592
ACC_TILES_WIDE_K = (496, 1024, 2)
ACC_TILES_TALL = (800, 512, 1)
ACC_COL_CHUNK = 512
ACC_ROW_CHUNK = 16
SEQ_TILE = 256
LRU_SEQ_TILE = 512
RET_HEAD_GROUP = 16
RET_SAMPLE_BLOCK = 2
LRU_GROUP = 4 * LRU_BLOCK
POOL_SEQ_BLOCK = 32

_LOG_G = np.log1p(-np.exp2(-5.0 - np.arange(RET_HEADS, dtype=np.float64)))


def _cparams(n_axes):
    return pltpu.CompilerParams(dimension_semantics=("arbitrary",) * n_axes,
                                vmem_limit_bytes=V7X_VMEM_LIMIT)


def _mm(a, b):
    return jnp.dot(a.astype(MXU_DTYPE), b.astype(MXU_DTYPE), preferred_element_type=F32)


def _mm_nt(a, b):
    return lax.dot_general(a.astype(MXU_DTYPE), b.astype(MXU_DTYPE), (((1,), (1,)), ((), ())),
                           preferred_element_type=F32)


def _mm_tn(a, b):
    return lax.dot_general(a.astype(MXU_DTYPE), b.astype(MXU_DTYPE), (((0,), (0,)), ((), ())),
                           preferred_element_type=F32)


def _layer_norm_rows(y, g, b):
    mu = jnp.mean(y, axis=-1, keepdims=True)
    d = y - mu
    var = jnp.mean(d * d, axis=-1, keepdims=True)
    return d * lax.rsqrt(var + LN_EPS) * g + b


def _rope_cols(acc, cos, sin):
    half = RET_DK // 2
    parts = []
    for h in range(acc.shape[1] // RET_DK):
        x1 = acc[:, h * RET_DK: h * RET_DK + half]
        x2 = acc[:, h * RET_DK + half: (h + 1) * RET_DK]
        parts.append(x1 * cos - x2 * sin)
        parts.append(x1 * sin + x2 * cos)
    return jnp.concatenate(parts, axis=1)


def _epi_q(accs, cos_ref, sin_ref, o_ref):
    o_ref[...] = _rope_cols(accs[0], cos_ref[...], sin_ref[...]).astype(o_ref.dtype)


def _epi_k(accs, cos_ref, sin_ref, e_ref, lg_ref, k_ref, kd_ref):
    k = _rope_cols(accs[0], cos_ref[...], sin_ref[...]) * (RET_DK ** -0.5)
    k_ref[...] = k.astype(k_ref.dtype)
    kd_ref[...] = (k * jnp.exp(e_ref[...] * lg_ref[...])).astype(kd_ref.dtype)


def _epi_cast(accs, o_ref):
    o_ref[...] = accs[0].astype(o_ref.dtype)


def _epi_silu(accs, o_ref):
    o_ref[...] = (accs[0] * jax.nn.sigmoid(accs[0])).astype(o_ref.dtype)


def _epi_gelu(accs, o_ref):
    o_ref[...] = jax.nn.gelu(accs[0]).astype(o_ref.dtype)


def _epi_swiglu(accs, o_ref):
    a, b = accs
    o_ref[...] = (a * jax.nn.sigmoid(a) * b).astype(o_ref.dtype)


def _project_kernel(x_ref, *refs, epilogue, n_w, has_side):
    w_refs, mid, w_mx = refs[:n_w], refs[n_w:len(refs) - n_w], refs[len(refs) - n_w:]
    if has_side:
        side_in, side_out, mid = mid[0], mid[-1], mid[1:-1]
        side_out[...] = side_in[...].astype(side_out.dtype)

    @pl.when(pl.program_id(1) == 0)
    def _():
        for w_ref, s_ref in zip(w_refs, w_mx):
            s_ref[...] = w_ref[...].astype(s_ref.dtype)

    x = x_ref[...]
    epilogue([jnp.dot(x, s_ref[...], preferred_element_type=F32) for s_ref in w_mx], *mid)


def _project(epilogue, x, weights, layer, col_offsets, n_cols, bn, bm, out_dtypes, row_inputs=(),
             col_inputs=(), side=None, name=None):
    rows, k = x.shape
    bm = min(bm, -(-rows // V7X_BF16_ROWS) * V7X_BF16_ROWS)
    assert all(off % bn == 0 for off in col_offsets)
    grid = (pl.cdiv(n_cols, bn), pl.cdiv(rows, bm))
    in_specs = [pl.BlockSpec((bm, k), lambda j, i: (i, 0))]
    for off in col_offsets:
        in_specs.append(pl.BlockSpec((None, k, bn),
                                     functools.partial(lambda j, i, o: (layer, 0, j + o), o=off // bn)))
    out_shape = [jax.ShapeDtypeStruct((rows, n_cols), dt) for dt in out_dtypes]
    out_specs = [pl.BlockSpec((bm, bn), lambda j, i: (i, j)) for _ in out_dtypes]
    operands = [x, *weights]
    if side is not None:
        src, side_layer = side
        n_side, d_side = src.shape[1:]
        slab = -(-pl.cdiv(n_side, grid[0] * grid[1]) // V7X_BF16_ROWS) * V7X_BF16_ROWS
        last_slab = pl.cdiv(n_side, slab) - 1
        ni = grid[1]
        in_specs.append(pl.BlockSpec((None, slab, d_side),
                                     lambda j, i: (side_layer, jnp.minimum(j * ni + i, last_slab), 0)))
        out_specs.append(pl.BlockSpec((slab, d_side), lambda j, i: (jnp.minimum(j * ni + i, last_slab), 0)))
        out_shape.append(jax.ShapeDtypeStruct((n_side, d_side), MXU_DTYPE))
        operands.append(src)
    for r in row_inputs:
        in_specs.append(pl.BlockSpec((bm, r.shape[1]), lambda j, i: (i, 0)))
    for c in col_inputs:
        in_specs.append(pl.BlockSpec((1, bn), lambda j, i: (0, j)))
    return pl.pallas_call(
        functools.partial(_project_kernel, epilogue=epilogue, n_w=len(weights), has_side=side is not None),
        grid=grid, in_specs=in_specs, out_specs=out_specs, out_shape=out_shape,
        scratch_shapes=[pltpu.VMEM((k, bn), MXU_DTYPE) for _ in weights],
        compiler_params=_cparams(2), name=name,
    )(*operands, *row_inputs, *col_inputs)


def _residual_ln_kernel(a_ref, w_ref, x_hbm, g_ref, b_ref, o_ref, obf_ref, xbuf, sem, *, rows, kdim):
    i, k = pl.program_id(0), pl.program_id(1)
    n_tiles, n_k = pl.num_programs(0), pl.num_programs(1)
    bm, d = o_ref.shape
    tail = rows - (pl.cdiv(rows, bm) - 1) * bm

    def with_x_copy(fn):
        def copy(size):
            return pltpu.make_async_copy(
                x_hbm.at[pl.ds(pl.multiple_of(i * bm, V7X_BF16_ROWS), size), :],
                xbuf.at[pl.ds(0, size), :], sem)
        if tail == bm:
            fn(copy(bm))
        else:
            pl.when(i < n_tiles - 1)(lambda: fn(copy(bm)))
            pl.when(i == n_tiles - 1)(lambda: fn(copy(tail)))

    def accumulate(first, k_valid):
        a = a_ref[:, :k_valid]
        for n in range(d // ACC_COL_CHUNK):
            cs = slice(n * ACC_COL_CHUNK, (n + 1) * ACC_COL_CHUNK)
            p = _mm(a, w_ref[:k_valid, cs])
            o_ref[:, cs] = p if first else o_ref[:, cs] + p

    bk = a_ref.shape[1]
    k_tail = kdim - (pl.cdiv(kdim, bk) - 1) * bk

    @pl.when(k == 0)
    def _():
        with_x_copy(lambda cp: cp.start())
        accumulate(True, bk)

    @pl.when(jnp.logical_and(k > 0, k < n_k - 1))
    def _():
        accumulate(False, bk)

    @pl.when(jnp.logical_and(k > 0, k == n_k - 1))
    def _():
        accumulate(False, k_tail)

    @pl.when(k == n_k - 1)
    def _():
        with_x_copy(lambda cp: cp.wait())
        rc = ACC_ROW_CHUNK
        for r in range(bm // rc):
            rs = slice(r * rc, (r + 1) * rc)
            out = _layer_norm_rows(ALPHA * xbuf[rs, :] + o_ref[rs, :], g_ref[...], b_ref[...])
            o_ref[rs, :] = out
            obf_ref[rs, :] = out.astype(obf_ref.dtype)


def _residual_ln(a, w, layer, x, g, b, bm, bk, out_buffers, name=None):
    rows, kdim = a.shape
    d = w.shape[2]
    bm = min(bm, -(-rows // V7X_BF16_ROWS) * V7X_BF16_ROWS)
    n_k = pl.cdiv(kdim, bk)
    k_tail = kdim - (n_k - 1) * bk
    assert bm % ACC_ROW_CHUNK == 0 and (n_k > 1 or kdim == bk) and k_tail % V7X_LANES == 0
    n_tiles = pl.cdiv(rows, bm)
    assert n_tiles > 1 or rows == bm
    out_spec = pl.BlockSpec((bm, d), lambda i, k: (i, 0), pipeline_mode=pl.Buffered(out_buffers))
    return pl.pallas_call(
        functools.partial(_residual_ln_kernel, rows=rows, kdim=kdim), grid=(n_tiles, n_k),
        in_specs=[pl.BlockSpec((bm, bk), lambda i, k: (i, k)),
                  pl.BlockSpec((None, bk, d), lambda i, k: (layer, k, 0)),
                  pl.BlockSpec(memory_space=pl.ANY),
                  pl.BlockSpec((1, d), lambda i, k: (0, 0)),
                  pl.BlockSpec((1, d), lambda i, k: (0, 0))],
        out_specs=[out_spec, out_spec],
        out_shape=[jax.ShapeDtypeStruct((rows, d), F32), jax.ShapeDtypeStruct((rows, d), MXU_DTYPE)],
        scratch_shapes=[pltpu.VMEM((bm, d), F32), pltpu.SemaphoreType.DMA],
        compiler_params=_cparams(2), name=name,
    )(a, w, x, g.reshape(1, d), b.reshape(1, d))


def _group_norm_gate(o, sg, gn):
    mu = jnp.mean(o, axis=-1, keepdims=True)
    d = o - mu
    var = jnp.mean(d * d, axis=-1, keepdims=True)
    return sg * (d * lax.rsqrt(var + LN_EPS) * gn)


def _ret_prompt_kernel(q_ref, k_ref, kd_ref, v_ref, sg_ref,
                       qm_ref, km_ref, kdm_ref, vm_ref, sgm_ref,
                       gn_ref, dec_ref, qd_ref, cd_ref, z_ref, zm_ref, st_ref):
    c = pl.program_id(2)
    heads = dec_ref.shape[0]

    @pl.when(c == 0)
    def _():
        for h in range(heads):
            cs = slice(h * RET_DK, (h + 1) * RET_DK)
            q, k, kd, v = qm_ref[:, cs], km_ref[:, cs], kdm_ref[:, cs], vm_ref[:, cs]
            scores = _mm_nt(q, k) * dec_ref[h, :N_META, :N_META]
            o = _mm(scores, v)
            zm_ref[:, cs] = _group_norm_gate(o, sgm_ref[:, cs], gn_ref[:, cs]).astype(zm_ref.dtype)
            st_ref[0, h] = _mm_tn(kd, v)

    for h in range(heads):
        cs = slice(h * RET_DK, (h + 1) * RET_DK)
        q, k, kd, v = q_ref[:, cs], k_ref[:, cs], kd_ref[:, cs], v_ref[:, cs]
        s = st_ref[0, h]
        scores = _mm_nt(q, k) * dec_ref[h]
        o = _mm(scores, v) + _mm(q, s) * qd_ref[h]
        z_ref[:, cs] = _group_norm_gate(o, sg_ref[:, cs], gn_ref[:, cs]).astype(z_ref.dtype)
        st_ref[0, h] = cd_ref[h] * s + _mm_tn(kd, v)


def _retention_tables(length):
    idx = np.arange(length, dtype=np.float64)
    diff = idx[:, None] - idx[None, :]
    dec = np.where(diff >= 0, np.exp(_LOG_G[:, None, None] * np.maximum(diff, 0.0)), 0.0)
    qd = np.exp(_LOG_G[:, None] * (idx[None, :] + 1.0))
    return dec.astype(np.float32), qd.astype(np.float32)


def _drop_ref(body, index):
    def wrapped(*refs):
        body(*refs[:index], *refs[index + 1:])
    return wrapped


def _retention_prompt(q, k, kd, v, sg, gn, batch, seq, off_meta, layer, n_layers, states):
    rows = q.shape[0]
    nc = seq // RET_CHUNK
    hg = RET_HEAD_GROUP
    wcols = hg * RET_DK
    dec, qd = _retention_tables(RET_CHUNK)
    qd_full = np.broadcast_to(qd[:, :, None], (RET_HEADS, RET_CHUNK, RET_DK)).copy()
    cd = np.broadcast_to(np.exp(_LOG_G * RET_CHUNK)[:, None, None], (RET_HEADS, 1, RET_DK)).astype(np.float32)
    assert off_meta % N_META == 0 and seq % RET_CHUNK == 0
    real = pl.BlockSpec((RET_CHUNK, wcols), lambda b, g, c: (b * nc + c, g))
    meta = pl.BlockSpec((N_META, wcols), lambda b, g, c: (off_meta // N_META + b, g))
    inputs = [q, k, kd, v, sg, q, k, kd, v, sg, gn.reshape(1, D_MODEL), jnp.asarray(dec), jnp.asarray(qd_full),
              jnp.asarray(cd)]
    in_specs = [real] * 5 + [meta] * 5 + [
        pl.BlockSpec((1, wcols), lambda b, g, c: (0, g)),
        pl.BlockSpec((hg, RET_CHUNK, RET_CHUNK), lambda b, g, c: (g, 0, 0)),
        pl.BlockSpec((hg, RET_CHUNK, RET_DK), lambda b, g, c: (g, 0, 0)),
        pl.BlockSpec((hg, 1, RET_DK), lambda b, g, c: (g, 0, 0))]
    body, aliases = _ret_prompt_kernel, {}
    if states is not None:
        body, aliases = _drop_ref(body, len(inputs)), {len(inputs): 2}
        inputs.append(states)
        in_specs.append(pl.BlockSpec(memory_space=pl.ANY))
    return pl.pallas_call(
        body, grid=(batch, RET_HEADS // hg, nc), in_specs=in_specs,
        out_specs=[real,
                   pl.BlockSpec((N_META, wcols), lambda b, g, c: (b, g)),
                   pl.BlockSpec((None, 1, hg, RET_DK, RET_DK), lambda b, g, c: (layer, b, g, 0, 0))],
        out_shape=[jax.ShapeDtypeStruct((rows, D_MODEL), MXU_DTYPE),
                   jax.ShapeDtypeStruct((batch * N_META, D_MODEL), MXU_DTYPE),
                   jax.ShapeDtypeStruct((n_layers, batch, RET_HEADS, RET_DK, RET_DK), F32)],
        input_output_aliases=aliases, compiler_params=_cparams(3), name="retention_prompt",
    )(*inputs)


def _ret_sample_kernel(q_ref, k_ref, kd_ref, v_ref, sg_ref, s0_ref, gn_ref, dec_ref, qd_ref, cd_ref,
                       z_ref, st_ref):
    for b in range(q_ref.shape[0]):
        for h in range(RET_HEADS):
            cs = slice(h * RET_DK, (h + 1) * RET_DK)
            q, k, kd, v = q_ref[b, :, cs], k_ref[b, :, cs], kd_ref[b, :, cs], v_ref[b, :, cs]
            s = s0_ref[b, h]
            scores = _mm_nt(q, k) * dec_ref[h]
            o = _mm(scores, v) + _mm(q, s) * qd_ref[h]
            z_ref[b, :, cs] = _group_norm_gate(o, sg_ref[b, :, cs], gn_ref[:, cs]).astype(z_ref.dtype)
            st_ref[b, h] = cd_ref[h] * s + _mm_tn(kd, v)


def _retention_sample(q, k, kd, v, sg, gn, state, layer, dec_seq, states):
    db, pad_rows, _ = q.shape
    dec, qd = _retention_tables(dec_seq)
    dec_p = np.zeros((RET_HEADS, pad_rows, pad_rows), np.float32)
    dec_p[:, :dec_seq, :dec_seq] = dec
    qd_p = np.zeros((RET_HEADS, pad_rows, RET_DK), np.float32)
    qd_p[:, :dec_seq, :] = qd[:, :, None]
    cd = np.broadcast_to(np.exp(_LOG_G * dec_seq)[:, None, None], (RET_HEADS, 1, RET_DK)).astype(np.float32)
    sb = RET_SAMPLE_BLOCK if db % RET_SAMPLE_BLOCK == 0 else 1
    tok = pl.BlockSpec((sb, pad_rows, D_MODEL), lambda b: (b, 0, 0))
    st = pl.BlockSpec((None, sb, RET_HEADS, RET_DK, RET_DK), lambda b: (layer, b, 0, 0, 0))
    inputs = [q, k, kd, v, sg, state, gn.reshape(1, D_MODEL), jnp.asarray(dec_p), jnp.asarray(qd_p), jnp.asarray(cd)]
    in_specs = [tok] * 5 + [
        st, pl.BlockSpec((1, D_MODEL), lambda b: (0, 0)),
        pl.BlockSpec((RET_HEADS, pad_rows, pad_rows), lambda b: (0, 0, 0)),
        pl.BlockSpec((RET_HEADS, pad_rows, RET_DK), lambda b: (0, 0, 0)),
        pl.BlockSpec((RET_HEADS, 1, RET_DK), lambda b: (0, 0, 0))]
    body, aliases = _ret_sample_kernel, {}
    if states is not None:
        body, aliases = _drop_ref(body, len(inputs)), {len(inputs): 1}
        inputs.append(states)
        in_specs.append(pl.BlockSpec(memory_space=pl.ANY))
    return pl.pallas_call(
        body, grid=(db // sb,), in_specs=in_specs, out_specs=[tok, st],
        out_shape=[jax.ShapeDtypeStruct((db, pad_rows, D_MODEL), MXU_DTYPE),
                   jax.ShapeDtypeStruct(state.shape, F32)],
        input_output_aliases=aliases, compiler_params=_cparams(1), name="retention_sample",
    )(*inputs)


def _shift_rows(x, prev8, j):
    xr = pltpu.roll(x, j, 0)
    pr = pltpu.roll(prev8, j, 0)
    row = lax.broadcasted_iota(jnp.int32, prev8.shape, 0)
    first = jnp.where(row < j, pr, xr[:V7X_SUBLANES])
    if x.shape[0] == V7X_SUBLANES:
        return first
    return jnp.concatenate([first, xr[V7X_SUBLANES:]], axis=0)


def _lru_gates(xc, wa, wx, ba, bx, cneg):
    r = jax.nn.sigmoid(_mm(xc, wa) + ba)
    i = jax.nn.sigmoid(_mm(xc, wx) + bx)
    log_a = cneg * r
    a = jnp.exp(log_a)
    z = 1.0 - a * a
    return a, jnp.where(z > 0.0, z * lax.rsqrt(z), 0.0) * (i * xc)


def _lru_tile(x, gate, hprev, xs_ref, sb_ref, cw, cb, wa, wx, ba, bx, cneg):
    n = x.shape[0]
    o = V7X_SUBLANES
    xs_ref[o:o + n, :] = x
    xc = cb + cw[3:4] * x
    for j in range(1, CONV_W):
        xc = xc + cw[3 - j:4 - j] * xs_ref[o - j:o - j + n, :]
    xs_ref[0:o, :] = x[n - o:]
    a, bb = _lru_gates(xc, wa, wx, ba, bx, cneg)
    groups = n // V7X_SUBLANES
    a = a.reshape(groups, V7X_SUBLANES, a.shape[1])
    bb = bb.reshape(groups, V7X_SUBLANES, bb.shape[1])
    row = lax.broadcasted_iota(jnp.int32, (1,) + a.shape[1:], 1)
    for s in (1, 2, 4):
        m = row >= s
        a_s = jnp.where(m, pltpu.roll(a, s, 1), 1.0)
        b_s = jnp.where(m, pltpu.roll(bb, s, 1), 0.0)
        bb = bb + a * b_s
        a = a * a_s
    h = hprev
    for g in range(groups):
        hg = bb[g] + a[g] * h
        sb_ref[o + g * V7X_SUBLANES:o + (g + 1) * V7X_SUBLANES, :] = hg
        h = hg[V7X_SUBLANES - 1:V7X_SUBLANES]
    return gate * sb_ref[o:o + n, :], h


def _lru_prompt_kernel(x_ref, gate_ref, xm_ref, gatem_ref, cw_ref, cb_ref, wa_ref, wx_ref, ba_ref, bx_ref,
                       cneg_ref, y_ref, ym_ref, hl_ref, h_ref, xs_ref, sb_ref):
    c = pl.program_id(2)
    scratch = (xs_ref, sb_ref)
    params = (cw_ref[...], cb_ref[...], wa_ref[...], wx_ref[...], ba_ref[...], bx_ref[...], cneg_ref[...])

    @pl.when(c == 0)
    def _():
        zeros8 = jnp.zeros(h_ref.shape, F32)
        for ref in scratch:
            ref[0:V7X_SUBLANES, :] = zeros8
        y, h = _lru_tile(xm_ref[...], gatem_ref[...], zeros8[0:1], *scratch, *params)
        ym_ref[...] = y.astype(ym_ref.dtype)
        h_ref[...] = jnp.broadcast_to(h, h_ref.shape)

    y, h = _lru_tile(x_ref[...], gate_ref[...], h_ref[0:1, :], *scratch, *params)
    y_ref[...] = y.astype(y_ref.dtype)
    h_ref[...] = jnp.broadcast_to(h, h_ref.shape)
    hl_ref[...] = h


def _lru_param_specs(imap):
    cg = LRU_GROUP
    return [pl.BlockSpec((CONV_W, cg), lambda *a: (0, imap(*a))),
            pl.BlockSpec((1, cg), lambda *a: (0, imap(*a))),
            pl.BlockSpec((None, cg, cg), lambda *a: (imap(*a), 0, 0)),
            pl.BlockSpec((None, cg, cg), lambda *a: (imap(*a), 0, 0)),
            pl.BlockSpec((1, cg), lambda *a: (0, imap(*a))),
            pl.BlockSpec((1, cg), lambda *a: (0, imap(*a))),
            pl.BlockSpec((1, cg), lambda *a: (0, imap(*a)))]


def _lru_prompt(xb, gate, params, batch, seq, off_meta):
    rows = xb.shape[0]
    cg = LRU_GROUP
    nt = seq // LRU_SEQ_TILE
    assert seq % LRU_SEQ_TILE == 0 and off_meta % N_META == 0
    real = pl.BlockSpec((LRU_SEQ_TILE, cg), lambda g, b, c: (b * nt + c, g))
    meta = pl.BlockSpec((N_META, cg), lambda g, b, c: (off_meta // N_META + b, g))
    return pl.pallas_call(
        _lru_prompt_kernel, grid=(D_RNN // cg, batch, nt),
        in_specs=[real, real, meta, meta] + _lru_param_specs(lambda g, b, c: g),
        out_specs=[real,
                   pl.BlockSpec((N_META, cg), lambda g, b, c: (b, g)),
                   pl.BlockSpec((None, 1, cg), lambda g, b, c: (b, 0, g))],
        out_shape=[jax.ShapeDtypeStruct((rows, D_RNN), MXU_DTYPE),
                   jax.ShapeDtypeStruct((batch * N_META, D_RNN), MXU_DTYPE),
                   jax.ShapeDtypeStruct((batch, 1, D_RNN), F32)],
        scratch_shapes=[pltpu.VMEM((V7X_SUBLANES, cg), F32)]
        + [pltpu.VMEM((V7X_SUBLANES + LRU_SEQ_TILE, cg), F32)] * 2,
        compiler_params=_cparams(3), name="rglru_prompt",
    )(xb, gate, xb, gate, *params)


def _lru_sample_kernel(x_ref, gate_ref, c0_ref, h0_ref, cw_ref, cb_ref, wa_ref, wx_ref, ba_ref, bx_ref,
                       cneg_ref, y_ref, hl_ref):
    ds = x_ref.shape[0]
    cw, cb = cw_ref[...], cb_ref[...]
    xp = [c0_ref[j] for j in range(CONV_W - 1)] + [x_ref[t] for t in range(ds)]
    xcs = []
    for t in range(ds):
        xc = cb + cw[0:1] * xp[t]
        for j in range(1, CONV_W):
            xc = xc + cw[j:j + 1] * xp[t + j]
        xcs.append(xc)
    xc = jnp.concatenate(xcs, axis=0)
    a, bb = _lru_gates(xc, wa_ref[...], wx_ref[...], ba_ref[...], bx_ref[...], cneg_ref[...])
    db = x_ref.shape[1]
    h = h0_ref[...]
    for t in range(ds):
        rs = slice(t * db, (t + 1) * db)
        h = a[rs] * h + bb[rs]
        y_ref[t] = (gate_ref[t] * h).astype(y_ref.dtype)
    hl_ref[...] = h


def _lru_sample(xb, gate, conv0, h0, params):
    ds, db, _ = xb.shape
    cg = LRU_GROUP
    tok = pl.BlockSpec((ds, db, cg), lambda g: (0, 0, g))
    return pl.pallas_call(
        _lru_sample_kernel, grid=(D_RNN // cg,),
        in_specs=[tok, tok,
                  pl.BlockSpec((CONV_W - 1, db, cg), lambda g: (0, 0, g)),
                  pl.BlockSpec((db, cg), lambda g: (0, g))] + _lru_param_specs(lambda g: g),
        out_specs=[tok, pl.BlockSpec((db, cg), lambda g: (0, g))],
        out_shape=[jax.ShapeDtypeStruct((ds, db, D_RNN), MXU_DTYPE),
                   jax.ShapeDtypeStruct((db, D_RNN), F32)],
        compiler_params=_cparams(1), name="rglru_sample",
    )(xb, gate, conv0, h0, *params)


def _pool_project(means, x, w_ref, scale, g, b):
    ys = []
    for gi in range(len(POOL_WINDOWS)):
        cs = slice(gi * POOL_GROUP, (gi + 1) * POOL_GROUP)
        ys.append(_mm(means[gi] - x[:, cs], w_ref[gi]))
    y = jnp.concatenate(ys, axis=1) * scale
    return _layer_norm_rows(ALPHA * x + y, g, b)


def _pool_tile(x, carry, pos0, w_ref, scale, g, b):
    n = x.shape[0]
    pg = POOL_GROUP

    def shift(v, prev8, k):
        if k < V7X_SUBLANES:
            return _shift_rows(v, prev8, k)
        if n == V7X_SUBLANES:
            return prev8
        return jnp.concatenate([prev8, v[:n - V7X_SUBLANES]], axis=0)

    levels = [x]
    for li, k in enumerate((1, 2, 4, 8)):
        v = levels[-1][:, pg:] if li > 0 else levels[-1]
        levels.append(v + shift(v, carry[li][:, D_MODEL - v.shape[1]:], k))
    pos = (pos0 + lax.broadcasted_iota(jnp.int32, (n, 1), 0)).astype(F32)
    means = []
    for gi, w in enumerate(POOL_WINDOWS):
        cnt = jnp.minimum(float(w), pos + 1.0)
        means.append(levels[gi + 1][:, :pg] / cnt)
    out = _pool_project(means, x, w_ref, scale, g, b)
    new_carry = []
    for li in range(4):
        v = levels[li]
        tail = v[n - V7X_SUBLANES:]
        if v.shape[1] < D_MODEL:
            tail = jnp.concatenate([jnp.zeros((V7X_SUBLANES, D_MODEL - v.shape[1]), F32), tail], axis=1)
        new_carry.append(tail)
    return out, new_carry


def _pool_prompt_kernel(x_ref, xm_ref, w_ref, sc_ref, g_ref, b_ref, o_ref, obf_ref, om_ref, ombf_ref,
                        carry_ref):
    c = pl.program_id(1)
    consts = (w_ref, sc_ref[...], g_ref[...], b_ref[...])

    @pl.when(c == 0)
    def _():
        zero = jnp.zeros((V7X_SUBLANES, D_MODEL), F32)
        out, carry = _pool_tile(xm_ref[...], [zero] * 4, 0, *consts)
        om_ref[...] = out
        ombf_ref[...] = out.astype(ombf_ref.dtype)
        for li in range(4):
            carry_ref[li] = carry[li]

    out, carry = _pool_tile(x_ref[...], [carry_ref[li] for li in range(4)],
                            N_META + c * x_ref.shape[0], *consts)
    o_ref[...] = out
    obf_ref[...] = out.astype(obf_ref.dtype)
    for li in range(4):
        carry_ref[li] = carry[li]


def _pool_prompt(x, w, scale, g, b, batch, seq, off_meta):
    rows = x.shape[0]
    nt = seq // SEQ_TILE
    ng = len(POOL_WINDOWS)
    assert seq % SEQ_TILE == 0 and off_meta % N_META == 0
    real = pl.BlockSpec((SEQ_TILE, D_MODEL), lambda bb, c: (bb * nt + c, 0))
    vec = pl.BlockSpec((1, D_MODEL), lambda bb, c: (0, 0))
    meta_out = pl.BlockSpec((N_META, D_MODEL), lambda bb, c: (bb, 0))
    return pl.pallas_call(
        _pool_prompt_kernel, grid=(batch, nt),
        in_specs=[real, pl.BlockSpec((N_META, D_MODEL), lambda bb, c: (off_meta // N_META + bb, 0)),
                  pl.BlockSpec((ng, POOL_GROUP, POOL_GROUP), lambda bb, c: (0, 0, 0)), vec, vec, vec],
        out_specs=[real, real, meta_out, meta_out],
        out_shape=[jax.ShapeDtypeStruct((rows, D_MODEL), F32), jax.ShapeDtypeStruct((rows, D_MODEL), MXU_DTYPE),
                   jax.ShapeDtypeStruct((batch * N_META, D_MODEL), F32),
                   jax.ShapeDtypeStruct((batch * N_META, D_MODEL), MXU_DTYPE)],
        scratch_shapes=[pltpu.VMEM((4, V7X_SUBLANES, D_MODEL), F32)],
        compiler_params=_cparams(2), name="pool_prompt",
    )(x, x, w, scale.reshape(1, D_MODEL), g.reshape(1, D_MODEL), b.reshape(1, D_MODEL))


def _pool_sample_kernel(x_ref, buf_ref, w_ref, sc_ref, g_ref, b_ref, o_ref, obf_ref, *, pos0):
    ds = x_ref.shape[0]
    nbuf = buf_ref.shape[0]
    pg = POOL_GROUP
    means = []
    for gi, w in enumerate(POOL_WINDOWS):
        cs = slice(gi * pg, (gi + 1) * pg)
        xp = [buf_ref[j, :, cs] for j in range(nbuf - (w - 1), nbuf)] + [x_ref[t, :, cs] for t in range(ds)]
        rows = []
        for t in range(ds):
            acc = xp[t]
            for j in range(1, w):
                acc = acc + xp[t + j]
            rows.append(acc / float(min(w, pos0 + t + 1)))
        means.append(jnp.concatenate(rows, axis=0))
    x = jnp.concatenate([x_ref[t] for t in range(ds)], axis=0)
    out = _pool_project(means, x, w_ref, sc_ref[...], g_ref[...], b_ref[...])
    bs = x_ref.shape[1]
    for t in range(ds):
        o_ref[t] = out[t * bs:(t + 1) * bs]
        obf_ref[t] = out[t * bs:(t + 1) * bs].astype(obf_ref.dtype)


def _pool_sample(x, buf, w, scale, g, b, pos0):
    ds, db, _ = x.shape
    nbuf = buf.shape[0]
    bs = min(POOL_SEQ_BLOCK, db)
    ng = len(POOL_WINDOWS)
    assert db % bs == 0
    tok = pl.BlockSpec((ds, bs, D_MODEL), lambda s: (0, s, 0))
    vec = pl.BlockSpec((1, D_MODEL), lambda s: (0, 0))
    return pl.pallas_call(
        functools.partial(_pool_sample_kernel, pos0=pos0), grid=(db // bs,),
        in_specs=[tok, pl.BlockSpec((nbuf, bs, D_MODEL), lambda s: (0, s, 0)),
                  pl.BlockSpec((ng, POOL_GROUP, POOL_GROUP), lambda s: (0, 0, 0)), vec, vec, vec],
        out_specs=[tok, tok],
        out_shape=[jax.ShapeDtypeStruct((ds, db, D_MODEL), F32), jax.ShapeDtypeStruct((ds, db, D_MODEL), MXU_DTYPE)],
        compiler_params=_cparams(1), name="pool_sample",
    )(x, buf, w, scale.reshape(1, D_MODEL), g.reshape(1, D_MODEL), b.reshape(1, D_MODEL))


def _block_diag_groups(w):
    per = LRU_GROUP // LRU_BLOCK
    groups = []
    for g in range(LRU_BLOCKS // per):
        rows = [jnp.pad(w[g * per + p], ((0, 0), (p * LRU_BLOCK, (per - 1 - p) * LRU_BLOCK))) for p in range(per)]
        groups.append(jnp.concatenate(rows, axis=0))
    return jnp.stack(groups)


def kernel(x_prompt, x_sample, state_ret, state_lru_h, state_lru_conv, state_pool, meta_tokens,
           ret_w_in, ret_gn_g, ret_w_out, lru_w_in, lru_conv_w, lru_conv_b, lru_w_a, lru_b_a,
           lru_w_x, lru_b_x, lru_lambda, lru_w_out, pool_w, pool_scale, ln_mix_g, ln_mix_b,
           ln_ffn_g, ln_ffn_b, ffn_w1, ffn_w3, ffn_w2):
    batch, seq, d = x_prompt.shape
    db, ds, _ = x_sample.shape
    assert d == D_MODEL
    n_real, n_samp, n_meta = batch * seq, db * ds, batch * N_META
    off_s, off_m = n_real, n_real + n_samp
    mx = MXU_DTYPE

    x = jnp.concatenate([
        x_prompt.reshape(n_real, d),
        jnp.swapaxes(x_sample, 0, 1).reshape(n_samp, d),
        jnp.broadcast_to(meta_tokens[None].astype(x_prompt.dtype), (batch, N_META, d)).reshape(n_meta, d),
    ], axis=0)
    xbf = x.astype(mx)

    t_real = jnp.tile(jnp.arange(seq, dtype=F32), batch)
    t_samp = jnp.repeat(jnp.arange(ds, dtype=F32), db)
    t_meta = jnp.tile(jnp.arange(N_META, dtype=F32), batch)
    pos = jnp.concatenate([N_META + t_real, PAST_LEN + t_samp, t_meta])
    half = RET_DK // 2
    inv = ROPE_BASE ** (-jnp.arange(half, dtype=F32) / half)
    ang = pos[:, None] * inv[None, :]
    cos_t, sin_t = jnp.cos(ang), jnp.sin(ang)
    e_rows = jnp.concatenate([
        (RET_CHUNK - 1.0) - jnp.mod(t_real, float(RET_CHUNK)), (ds - 1.0) - t_samp, (N_META - 1.0) - t_meta,
    ])[:, None]
    lg_cols = jnp.repeat(jnp.asarray(_LOG_G, F32), RET_DK)[None, :]

    def sample_rows(a):
        return a[off_s:off_m].reshape(ds, db, a.shape[1])

    def merge(full, samp, meta):
        full = lax.dynamic_update_slice(full, samp.reshape(n_samp, samp.shape[-1]), (off_s, 0))
        return lax.dynamic_update_slice(full, meta, (off_m, 0))

    pad_rows = -(-ds // V7X_BF16_ROWS) * V7X_BF16_ROWS

    def to_seq_major(a):
        a = jnp.swapaxes(a, 0, 1)
        return jnp.pad(a, ((0, 0), (0, pad_rows - ds), (0, 0)))

    def last_rows(a, n):
        return jnp.stack([a[(b + 1) * seq - n:(b + 1) * seq] for b in range(batch)])

    def stacked_rows(w):
        return w.reshape(1, w.shape[0] * w.shape[1], w.shape[2])

    w2_mx = {}

    n_ret = ret_w_in.shape[0]
    ret_p = ret_s = None
    outs_h_p, outs_h_s, outs_conv_p, outs_conv_s, outs_pool_p, outs_pool_s = [], [], [], [], [], []

    for i in range(DEPTH):
        j, kind = i // 3, i % 3
        if kind == 0:
            first = j == 0
            q, w2_mx[i] = _project(_epi_q, xbf, [ret_w_in], j, [0], D_MODEL, 512, ROW_TILE, [mx],
                                   row_inputs=[cos_t, sin_t], side=(ffn_w2, i), name="ret_q")
            k, kd, *side_k = _project(_epi_k, xbf, [ret_w_in], j, [D_MODEL], D_MODEL, 512, ROW_TILE, [mx, mx],
                                      row_inputs=[cos_t, sin_t, e_rows], col_inputs=[lg_cols],
                                      side=(stacked_rows(lru_w_out), 0) if first else None, name="ret_k")
            v, *side_v = _project(_epi_cast, xbf, [ret_w_in], j, [2 * D_MODEL], D_MODEL, 512, ROW_TILE, [mx],
                                  side=(ffn_w2, i + 2) if first and i + 2 < DEPTH else None, name="ret_v")
            sg, *side_g = _project(_epi_silu, xbf, [ret_w_in], j, [3 * D_MODEL], D_MODEL, 512, ROW_TILE, [F32],
                                   side=(stacked_rows(ret_w_out), 0) if first else None, name="ret_g")
            if first:
                lru_w_out_mx = side_k[0].reshape(lru_w_out.shape)
                ret_w_out_mx = side_g[0].reshape(ret_w_out.shape)
                if side_v:
                    w2_mx[i + 2] = side_v[0]
            z, z_meta, ret_p = _retention_prompt(q, k, kd, v, sg, ret_gn_g[j], batch, seq, off_m, j, n_ret, ret_p)
            z_s, ret_s = _retention_sample(*(to_seq_major(sample_rows(a)) for a in (q, k, kd, v, sg)),
                                           ret_gn_g[j], state_ret, j, ds, ret_s)
            z = merge(z, jnp.swapaxes(z_s[:, :ds], 0, 1), z_meta)
            x, xbf = _residual_ln(z, ret_w_out_mx, j, x, ln_mix_g[i], ln_mix_b[i], *ACC_TILES_WIDE_K, name="ret_out")
        elif kind == 1:
            gate, w2_mx[i] = _project(_epi_gelu, xbf, [lru_w_in], j, [0], D_RNN, 512, ROW_TILE, [F32],
                                      side=(ffn_w2, i), name="lru_gate")
            (xb,) = _project(_epi_cast, xbf, [lru_w_in], j, [D_RNN], D_RNN, 512, ROW_TILE, [F32], name="lru_x")
            params = (lru_conv_w[j], lru_conv_b[j].reshape(1, D_RNN),
                      _block_diag_groups(lru_w_a[j]).astype(mx), _block_diag_groups(lru_w_x[j]).astype(mx),
                      lru_b_a[j].reshape(1, D_RNN), lru_b_x[j].reshape(1, D_RNN),
                      (-LRU_C * jax.nn.softplus(-lru_lambda[j].astype(F32))).reshape(1, D_RNN))
            y, y_meta, h_p = _lru_prompt(xb, gate, params, batch, seq, off_m)
            xb_s = sample_rows(xb)
            y_s, h_s = _lru_sample(xb_s, sample_rows(gate), jnp.swapaxes(state_lru_conv[j], 0, 1).astype(F32),
                                   state_lru_h[j].astype(F32), params)
            y = merge(y, y_s, y_meta)
            x, xbf = _residual_ln(y, lru_w_out_mx, j, x, ln_mix_g[i], ln_mix_b[i], *ACC_TILES_TALL, name="lru_out")
            outs_h_p.append(h_p.reshape(batch, D_RNN))
            outs_h_s.append(h_s)
            assert seq >= CONV_W - 1
            outs_conv_p.append(last_rows(xb, CONV_W - 1))
            xp_s = jnp.concatenate([state_lru_conv[j].astype(F32), jnp.swapaxes(xb_s, 0, 1)], axis=1)
            outs_conv_s.append(xp_s[:, -(CONV_W - 1):])
        else:
            p = POOL_MAX - 1
            w_pool = pool_w[j].astype(mx)
            x_s = sample_rows(x)
            assert seq >= p
            outs_pool_p.append(last_rows(x, p))
            xp_s = jnp.concatenate([state_pool[j].astype(F32), jnp.swapaxes(x_s, 0, 1)], axis=1)
            outs_pool_s.append(xp_s[:, -p:])
            xo, xobf, xm, xmbf = _pool_prompt(x, w_pool, pool_scale[j], ln_mix_g[i], ln_mix_b[i], batch, seq, off_m)
            buf_tm = jnp.swapaxes(state_pool[j].astype(F32), 0, 1)
            xs, xsbf = _pool_sample(x_s, buf_tm, w_pool, pool_scale[j], ln_mix_g[i], ln_mix_b[i], PAST_LEN)
            x, xbf = merge(xo, xs, xm), merge(xobf, xsbf, xmbf)
        (hid,) = _project(_epi_swiglu, xbf, [ffn_w1, ffn_w3], i, [0, 0], D_FF, 512, FFN_ROW_TILE, [mx],
                          name="ffn_up")
        x, xbf = _residual_ln(hid, w2_mx[i][None], 0, x, ln_ffn_g[i], ln_ffn_b[i], *ACC_TILES_WIDE_K,
                              name="ffn_down")

    y_prompt = x[:n_real].reshape(batch, seq, d)
    y_sample = jnp.swapaxes(x[off_s:off_m].reshape(ds, db, d), 0, 1)
    return (y_prompt, y_sample, ret_p, ret_s, jnp.stack(outs_h_p),
            jnp.stack(outs_h_s), jnp.stack(outs_conv_p), jnp.stack(outs_conv_s), jnp.stack(outs_pool_p),
            jnp.stack(outs_pool_s))
```

```python
import functools

import numpy as np
import jax
import jax.numpy as jnp
from jax import lax
from jax.experimental import pallas as pl
from jax.experimental.pallas import tpu as pltpu

F32 = jnp.float32
MXU_DTYPE = jnp.bfloat16

D_MODEL = 4096
N_META = 16
PAST_LEN = 16384
RET_HEADS = 16
RET_DK = D_MODEL // RET_HEADS
RET_CHUNK = 128
ROPE_BASE = 10000.0
D_RNN = 5632
LRU_BLOCKS = 16
LRU_BLOCK = D_RNN // LRU_BLOCKS
CONV_W = 4
LRU_C = 8.0
POOL_WINDOWS = (2, 4, 8, 16)
POOL_GROUP = D_MODEL // len(POOL_WINDOWS)
POOL_MAX = 16
D_FF = 11008
DEPTH = 4
ALPHA = (2.0 * DEPTH) ** 0.25
LN_EPS = 1e-5

V7X_SUBLANES = 8
V7X_LANES = 128
V7X_BF16_ROWS = 16
V7X_VMEM_LIMIT = 58 * 2 ** 20

ROW_TILE = 1104
FFN_ROW_TILE = 592
ACC_TILES_WIDE_K = (496, 1024, 2)
ACC_COL_CHUNK = 512
ACC_ROW_CHUNK = 16
SEQ_TILE = 256
LRU_SEQ_TILE = 512
RET_HEAD_GROUP = 16
RET_SAMPLE_BLOCK = 2
LRU_GROUP = 4 * LRU_BLOCK
POOL_SEQ_BLOCK = 32

_LOG_G = np.log1p(-np.exp2(-5.0 - np.arange(RET_HEADS, dtype=np.float64)))


def _cparams(n_axes):
    return pltpu.CompilerParams(dimension_semantics=("arbitrary",) * n_axes,
                                vmem_limit_bytes=V7X_VMEM_LIMIT)


def _mm(a, b):
    return jnp.dot(a.astype(MXU_DTYPE), b.astype(MXU_DTYPE), preferred_element_type=F32)


def _mm_nt(a, b):
    return lax.dot_general(a.astype(MXU_DTYPE), b.astype(MXU_DTYPE), (((1,), (1,)), ((), ())),
                           preferred_element_type=F32)


def _mm_tn(a, b):
    return lax.dot_general(a.astype(MXU_DTYPE), b.astype(MXU_DTYPE), (((0,), (0,)), ((), ())),
                           preferred_element_type=F32)


def _layer_norm_rows(y, g, b):
    mu = jnp.mean(y, axis=-1, keepdims=True)
    d = y - mu
    var = jnp.mean(d * d, axis=-1, keepdims=True)
    return d * lax.rsqrt(var + LN_EPS) * g + b


def _rope_cols(acc, cos, sin):
    half = RET_DK // 2
    parts = []
    for h in range(acc.shape[1] // RET_DK):
        x1 = acc[:, h * RET_DK: h * RET_DK + half]
        x2 = acc[:, h * RET_DK + half: (h + 1) * RET_DK]
        parts.append(x1 * cos - x2 * sin)
        parts.append(x1 * sin + x2 * cos)
    return jnp.concatenate(parts, axis=1)


def _epi_q(accs, cos_ref, sin_ref, o_ref):
    o_ref[...] = _rope_cols(accs[0], cos_ref[...], sin_ref[...]).astype(o_ref.dtype)


def _epi_k(accs, cos_ref, sin_ref, e_ref, lg_ref, k_ref, kd_ref):
    k = _rope_cols(accs[0], cos_ref[...], sin_ref[...]) * (RET_DK ** -0.5)
    k_ref[...] = k.astype(k_ref.dtype)
    kd_ref[...] = (k * jnp.exp(e_ref[...] * lg_ref[...])).astype(kd_ref.dtype)


def _epi_cast(accs, o_ref):
    o_ref[...] = accs[0].astype(o_ref.dtype)


def _epi_silu(accs, o_ref):
    o_ref[...] = (accs[0] * jax.nn.sigmoid(accs[0])).astype(o_ref.dtype)


def _epi_gelu(accs, o_ref):
    o_ref[...] = jax.nn.gelu(accs[0]).astype(o_ref.dtype)


def _epi_swiglu(accs, o_ref):
    a, b = accs
    o_ref[...] = (a * jax.nn.sigmoid(a) * b).astype(o_ref.dtype)


def _project_kernel(x_ref, *refs, epilogue, n_w, has_side):
    w_refs, mid, w_mx = refs[:n_w], refs[n_w:len(refs) - n_w], refs[len(refs) - n_w:]
    if has_side:
        side_in, side_out, mid = mid[0], mid[-1], mid[1:-1]
        side_out[...] = side_in[...].astype(side_out.dtype)

    @pl.when(pl.program_id(1) == 0)
    def _():
        for w_ref, s_ref in zip(w_refs, w_mx):
            s_ref[...] = w_ref[...].astype(s_ref.dtype)

    x = x_ref[...]
    epilogue([jnp.dot(x, s_ref[...], preferred_element_type=F32) for s_ref in w_mx], *mid)


def _project(epilogue, x, weights, layer, col_offsets, n_cols, bn, bm, out_dtypes, row_inputs=(),
             col_inputs=(), side=None, name=None):
    rows, k = x.shape
    bm = min(bm, -(-rows // V7X_BF16_ROWS) * V7X_BF16_ROWS)
    assert all(off % bn == 0 for off in col_offsets)
    grid = (pl.cdiv(n_cols, bn), pl.cdiv(rows, bm))
    in_specs = [pl.BlockSpec((bm, k), lambda j, i: (i, 0))]
    for off in col_offsets:
        in_specs.append(pl.BlockSpec((None, k, bn),
                                     functools.partial(lambda j, i, o: (layer, 0, j + o), o=off // bn)))
    out_shape = [jax.ShapeDtypeStruct((rows, n_cols), dt) for dt in out_dtypes]
    out_specs = [pl.BlockSpec((bm, bn), lambda j, i: (i, j)) for _ in out_dtypes]
    operands = [x, *weights]
    if side is not None:
        src, side_layer = side
        n_side, d_side = src.shape[1:]
        slab = -(-pl.cdiv(n_side, grid[0] * grid[1]) // V7X_BF16_ROWS) * V7X_BF16_ROWS
        last_slab = pl.cdiv(n_side, slab) - 1
        ni = grid[1]
        in_specs.append(pl.BlockSpec((None, slab, d_side),
                                     lambda j, i: (side_layer, jnp.minimum(j * ni + i, last_slab), 0)))
        out_specs.append(pl.BlockSpec((slab, d_side), lambda j, i: (jnp.minimum(j * ni + i, last_slab), 0)))
        out_shape.append(jax.ShapeDtypeStruct((n_side, d_side), MXU_DTYPE))
        operands.append(src)
    for r in row_inputs:
        in_specs.append(pl.BlockSpec((bm, r.shape[1]), lambda j, i: (i, 0)))
    for c in col_inputs:
        in_specs.append(pl.BlockSpec((1, bn), lambda j, i: (0, j)))
    return pl.pallas_call(
        functools.partial(_project_kernel, epilogue=epilogue, n_w=len(weights), has_side=side is not None),
        grid=grid, in_specs=in_specs, out_specs=out_specs, out_shape=out_shape,
        scratch_shapes=[pltpu.VMEM((k, bn), MXU_DTYPE) for _ in weights],
        compiler_params=_cparams(2), name=name,
    )(*operands, *row_inputs, *col_inputs)


def _residual_ln_kernel(a_ref, w_ref, x_hbm, g_ref, b_ref, o_ref, obf_ref, xbuf, sem, *, rows, kdim):
    i, k = pl.program_id(0), pl.program_id(1)
    n_tiles, n_k = pl.num_programs(0), pl.num_programs(1)
    bm, d = o_ref.shape
    tail = rows - (pl.cdiv(rows, bm) - 1) * bm

    def with_x_copy(fn):
        def copy(size):
            return pltpu.make_async_copy(
                x_hbm.at[pl.ds(pl.multiple_of(i * bm, V7X_BF16_ROWS), size), :],
                xbuf.at[pl.ds(0, size), :], sem)
        if tail == bm:
            fn(copy(bm))
        else:
            pl.when(i < n_tiles - 1)(lambda: fn(copy(bm)))
            pl.when(i == n_tiles - 1)(lambda: fn(copy(tail)))

    def accumulate(first, k_valid):
        a = a_ref[:, :k_valid]
        for n in range(d // ACC_COL_CHUNK):
            cs = slice(n * ACC_COL_CHUNK, (n + 1) * ACC_COL_CHUNK)
            p = _mm(a, w_ref[:k_valid, cs])
            o_ref[:, cs] = p if first else o_ref[:, cs] + p

    bk = a_ref.shape[1]
    k_tail = kdim - (pl.cdiv(kdim, bk) - 1) * bk

    @pl.when(k == 0)
    def _():
        with_x_copy(lambda cp: cp.start())
        accumulate(True, bk)

    @pl.when(jnp.logical_and(k > 0, k < n_k - 1))
    def _():
        accumulate(False, bk)

    @pl.when(jnp.logical_and(k > 0, k == n_k - 1))
    def _():
        accumulate(False, k_tail)

    @pl.when(k == n_k - 1)
    def _():
        with_x_copy(lambda cp: cp.wait())
        rc = ACC_ROW_CHUNK
        for r in range(bm // rc):
            rs = slice(r * rc, (r + 1) * rc)
            out = _layer_norm_rows(ALPHA * xbuf[rs, :] + o_ref[rs, :], g_ref[...], b_ref[...])
            o_ref[rs, :] = out
            obf_ref[rs, :] = out.astype(obf_ref.dtype)


def _residual_ln(a, w, layer, x, g, b, bm, bk, out_buffers, name=None):
    rows, kdim = a.shape
    d = w.shape[2]
    bm = min(bm, -(-rows // V7X_BF16_ROWS) * V7X_BF16_ROWS)
    n_k = pl.cdiv(kdim, bk)
    k_tail = kdim - (n_k - 1) * bk
    assert bm % ACC_ROW_CHUNK == 0 and (n_k > 1 or kdim == bk) and k_tail % V7X_LANES == 0
    n_tiles = pl.cdiv(rows, bm)
    assert n_tiles > 1 or rows == bm
    out_spec = pl.BlockSpec((bm, d), lambda i, k: (i, 0), pipeline_mode=pl.Buffered(out_buffers))
    return pl.pallas_call(
        functools.partial(_residual_ln_kernel, rows=rows, kdim=kdim), grid=(n_tiles, n_k),
        in_specs=[pl.BlockSpec((bm, bk), lambda i, k: (i, k)),
                  pl.BlockSpec((None, bk, d), lambda i, k: (layer, k, 0)),
                  pl.BlockSpec(memory_space=pl.ANY),
                  pl.BlockSpec((1, d), lambda i, k: (0, 0)),
                  pl.BlockSpec((1, d), lambda i, k: (0, 0))],
        out_specs=[out_spec, out_spec],
        out_shape=[jax.ShapeDtypeStruct((rows, d), F32), jax.ShapeDtypeStruct((rows, d), MXU_DTYPE)],
        scratch_shapes=[pltpu.VMEM((bm, d), F32), pltpu.SemaphoreType.DMA],
        compiler_params=_cparams(2), name=name,
    )(a, w, x, g.reshape(1, d), b.reshape(1, d))


def _group_norm_gate(o, sg, gn):
    mu = jnp.mean(o, axis=-1, keepdims=True)
    d = o - mu
    var = jnp.mean(d * d, axis=-1, keepdims=True)
    return sg * (d * lax.rsqrt(var + LN_EPS) * gn)


def _ret_prompt_kernel(q_ref, k_ref, kd_ref, v_ref, sg_ref,
                       qm_ref, km_ref, kdm_ref, vm_ref, sgm_ref,
                       gn_ref, dec_ref, qd_ref, cd_ref, z_ref, zm_ref, st_ref):
    c = pl.program_id(2)
    heads = dec_ref.shape[0]

    @pl.when(c == 0)
    def _():
        for h in range(heads):
            cs = slice(h * RET_DK, (h + 1) * RET_DK)
            q, k, kd, v = qm_ref[:, cs], km_ref[:, cs], kdm_ref[:, cs], vm_ref[:, cs]
            scores = _mm_nt(q, k) * dec_ref[h, :N_META, :N_META]
            o = _mm(scores, v)
            zm_ref[:, cs] = _group_norm_gate(o, sgm_ref[:, cs], gn_ref[:, cs]).astype(zm_ref.dtype)
            st_ref[0, h] = _mm_tn(kd, v)

    for h in range(heads):
        cs = slice(h * RET_DK, (h + 1) * RET_DK)
        q, k, kd, v = q_ref[:, cs], k_ref[:, cs], kd_ref[:, cs], v_ref[:, cs]
        s = st_ref[0, h]
        scores = _mm_nt(q, k) * dec_ref[h]
        o = _mm(scores, v) + _mm(q, s) * qd_ref[h]
        z_ref[:, cs] = _group_norm_gate(o, sg_ref[:, cs], gn_ref[:, cs]).astype(z_ref.dtype)
        st_ref[0, h] = cd_ref[h] * s + _mm_tn(kd, v)


def _retention_tables(length):
    idx = np.arange(length, dtype=np.float64)
    diff = idx[:, None] - idx[None, :]
    dec = np.where(diff >= 0, np.exp(_LOG_G[:, None, None] * np.maximum(diff, 0.0)), 0.0)
    qd = np.exp(_LOG_G[:, None] * (idx[None, :] + 1.0))
    return dec.astype(np.float32), qd.astype(np.float32)


def _drop_ref(body, index):
    def wrapped(*refs):
        body(*refs[:index], *refs[index + 1:])
    return wrapped


def _retention_prompt(q, k, kd, v, sg, gn, batch, seq, off_meta, layer, n_layers, states):
    rows = q.shape[0]
    nc = seq // RET_CHUNK
    hg = RET_HEAD_GROUP
    wcols = hg * RET_DK
    dec, qd = _retention_tables(RET_CHUNK)
    qd_full = np.broadcast_to(qd[:, :, None], (RET_HEADS, RET_CHUNK, RET_DK)).copy()
    cd = np.broadcast_to(np.exp(_LOG_G * RET_CHUNK)[:, None, None], (RET_HEADS, 1, RET_DK)).astype(np.float32)
    assert off_meta % N_META == 0 and seq % RET_CHUNK == 0
    real = pl.BlockSpec((RET_CHUNK, wcols), lambda b, g, c: (b * nc + c, g))
    meta = pl.BlockSpec((N_META, wcols), lambda b, g, c: (off_meta // N_META + b, g))
    inputs = [q, k, kd, v, sg, q, k, kd, v, sg, gn.reshape(1, D_MODEL), jnp.asarray(dec), jnp.asarray(qd_full),
              jnp.asarray(cd)]
    in_specs = [real] * 5 + [meta] * 5 + [
        pl.BlockSpec((1, wcols), lambda b, g, c: (0, g)),
        pl.BlockSpec((hg, RET_CHUNK, RET_CHUNK), lambda b, g, c: (g, 0, 0)),
        pl.BlockSpec((hg, RET_CHUNK, RET_DK), lambda b, g, c: (g, 0, 0)),
        pl.BlockSpec((hg, 1, RET_DK), lambda b, g, c: (g, 0, 0))]
    body, aliases = _ret_prompt_kernel, {}
    if states is not None:
        body, aliases = _drop_ref(body, len(inputs)), {len(inputs): 2}
        inputs.append(states)
        in_specs.append(pl.BlockSpec(memory_space=pl.ANY))
    return pl.pallas_call(
        body, grid=(batch, RET_HEADS // hg, nc), in_specs=in_specs,
        out_specs=[real,
                   pl.BlockSpec((N_META, wcols), lambda b, g, c: (b, g)),
                   pl.BlockSpec((None, 1, hg, RET_DK, RET_DK), lambda b, g, c: (layer, b, g, 0, 0))],
        out_shape=[jax.ShapeDtypeStruct((rows, D_MODEL), MXU_DTYPE),
                   jax.ShapeDtypeStruct((batch * N_META, D_MODEL), MXU_DTYPE),
                   jax.ShapeDtypeStruct((n_layers, batch, RET_HEADS, RET_DK, RET_DK), F32)],
        input_output_aliases=aliases, compiler_params=_cparams(3), name="retention_prompt",
    )(*inputs)


def _ret_sample_kernel(q_ref, k_ref, kd_ref, v_ref, sg_ref, s0_ref, gn_ref, dec_ref, qd_ref, cd_ref,
                       z_ref, st_ref):
    for b in range(q_ref.shape[0]):
        for h in range(RET_HEADS):
            cs = slice(h * RET_DK, (h + 1) * RET_DK)
            q, k, kd, v = q_ref[b, :, cs], k_ref[b, :, cs], kd_ref[b, :, cs], v_ref[b, :, cs]
            s = s0_ref[b, h]
            scores = _mm_nt(q, k) * dec_ref[h]
            o = _mm(scores, v) + _mm(q, s) * qd_ref[h]
            z_ref[b, :, cs] = _group_norm_gate(o, sg_ref[b, :, cs], gn_ref[:, cs]).astype(z_ref.dtype)
            st_ref[b, h] = cd_ref[h] * s + _mm_tn(kd, v)


def _retention_sample(q, k, kd, v, sg, gn, state, layer, dec_seq, states):
    db, pad_rows, _ = q.shape
    dec, qd = _retention_tables(dec_seq)
    dec_p = np.zeros((RET_HEADS, pad_rows, pad_rows), np.float32)
    dec_p[:, :dec_seq, :dec_seq] = dec
    qd_p = np.zeros((RET_HEADS, pad_rows, RET_DK), np.float32)
    qd_p[:, :dec_seq, :] = qd[:, :, None]
    cd = np.broadcast_to(np.exp(_LOG_G * dec_seq)[:, None, None], (RET_HEADS, 1, RET_DK)).astype(np.float32)
    sb = RET_SAMPLE_BLOCK if db % RET_SAMPLE_BLOCK == 0 else 1
    tok = pl.BlockSpec((sb, pad_rows, D_MODEL), lambda b: (b, 0, 0))
    st = pl.BlockSpec((None, sb, RET_HEADS, RET_DK, RET_DK), lambda b: (layer, b, 0, 0, 0))
    inputs = [q, k, kd, v, sg, state, gn.reshape(1, D_MODEL), jnp.asarray(dec_p), jnp.asarray(qd_p), jnp.asarray(cd)]
    in_specs = [tok] * 5 + [
        st, pl.BlockSpec((1, D_MODEL), lambda b: (0, 0)),
        pl.BlockSpec((RET_HEADS, pad_rows, pad_rows), lambda b: (0, 0, 0)),
        pl.BlockSpec((RET_HEADS, pad_rows, RET_DK), lambda b: (0, 0, 0)),
        pl.BlockSpec((RET_HEADS, 1, RET_DK), lambda b: (0, 0, 0))]
    body, aliases = _ret_sample_kernel, {}
    if states is not None:
        body, aliases = _drop_ref(body, len(inputs)), {len(inputs): 1}
        inputs.append(states)
        in_specs.append(pl.BlockSpec(memory_space=pl.ANY))
    return pl.pallas_call(
        body, grid=(db // sb,), in_specs=in_specs, out_specs=[tok, st],
        out_shape=[jax.ShapeDtypeStruct((db, pad_rows, D_MODEL), MXU_DTYPE),
                   jax.ShapeDtypeStruct(state.shape, F32)],
        input_output_aliases=aliases, compiler_params=_cparams(1), name="retention_sample",
    )(*inputs)


def _shift_rows(x, prev8, j):
    xr = pltpu.roll(x, j, 0)
    pr = pltpu.roll(prev8, j, 0)
    row = lax.broadcasted_iota(jnp.int32, prev8.shape, 0)
    first = jnp.where(row < j, pr, xr[:V7X_SUBLANES])
    if x.shape[0] == V7X_SUBLANES:
        return first
    return jnp.concatenate([first, xr[V7X_SUBLANES:]], axis=0)


def _lru_gates(xc, wa, wx, ba, bx, cneg):
    r = jax.nn.sigmoid(_mm(xc, wa) + ba)
    i = jax.nn.sigmoid(_mm(xc, wx) + bx)
    log_a = cneg * r
    a = jnp.exp(log_a)
    z = 1.0 - a * a
    return a, jnp.where(z > 0.0, z * lax.rsqrt(z), 0.0) * (i * xc)


def _lru_tile(x, gate, hprev, xs_ref, sb_ref, cw, cb, wa, wx, ba, bx, cneg):
    n = x.shape[0]
    o = V7X_SUBLANES
    xs_ref[o:o + n, :] = x
    xc = cb + cw[3:4] * x
    for j in range(1, CONV_W):
        xc = xc + cw[3 - j:4 - j] * xs_ref[o - j:o - j + n, :]
    xs_ref[0:o, :] = x[n - o:]
    a, bb = _lru_gates(xc, wa, wx, ba, bx, cneg)
    groups = n // V7X_SUBLANES
    a = a.reshape(groups, V7X_SUBLANES, a.shape[1])
    bb = bb.reshape(groups, V7X_SUBLANES, bb.shape[1])
    row = lax.broadcasted_iota(jnp.int32, (1,) + a.shape[1:], 1)
    for s in (1, 2, 4):
        m = row >= s
        a_s = jnp.where(m, pltpu.roll(a, s, 1), 1.0)
        b_s = jnp.where(m, pltpu.roll(bb, s, 1), 0.0)
        bb = bb + a * b_s
        a = a * a_s
    h = hprev
    for g in range(groups):
        hg = bb[g] + a[g] * h
        sb_ref[o + g * V7X_SUBLANES:o + (g + 1) * V7X_SUBLANES, :] = hg
        h = hg[V7X_SUBLANES - 1:V7X_SUBLANES]
    return gate * sb_ref[o:o + n, :], h


def _lru_prompt_kernel(x_ref, gate_ref, xm_ref, gatem_ref, cw_ref, cb_ref, wa_ref, wx_ref, ba_ref, bx_ref,
                       cneg_ref, y_ref, ym_ref, hl_ref, h_ref, xs_ref, sb_ref):
    c = pl.program_id(2)
    scratch = (xs_ref, sb_ref)
    params = (cw_ref[...], cb_ref[...], wa_ref[...], wx_ref[...], ba_ref[...], bx_ref[...], cneg_ref[...])

    @pl.when(c == 0)
    def _():
        zeros8 = jnp.zeros(h_ref.shape, F32)
        for ref in scratch:
            ref[0:V7X_SUBLANES, :] = zeros8
        y, h = _lru_tile(xm_ref[...], gatem_ref[...], zeros8[0:1], *scratch, *params)
        ym_ref[...] = y.astype(ym_ref.dtype)
        h_ref[...] = jnp.broadcast_to(h, h_ref.shape)

    y, h = _lru_tile(x_ref[...], gate_ref[...], h_ref[0:1, :], *scratch, *params)
    y_ref[...] = y.astype(y_ref.dtype)
    h_ref[...] = jnp.broadcast_to(h, h_ref.shape)
    hl_ref[...] = h


def _lru_param_specs(imap):
    cg = LRU_GROUP
    return [pl.BlockSpec((CONV_W, cg), lambda *a: (0, imap(*a))),
            pl.BlockSpec((1, cg), lambda *a: (0, imap(*a))),
            pl.BlockSpec((None, cg, cg), lambda *a: (imap(*a), 0, 0)),
            pl.BlockSpec((None, cg, cg), lambda *a: (imap(*a), 0, 0)),
            pl.BlockSpec((1, cg), lambda *a: (0, imap(*a))),
            pl.BlockSpec((1, cg), lambda *a: (0, imap(*a))),
            pl.BlockSpec((1, cg), lambda *a: (0, imap(*a)))]


def _lru_prompt(xb, gate, params, batch, seq, off_meta):
    rows = xb.shape[0]
    cg = LRU_GROUP
    nt = seq // LRU_SEQ_TILE
    assert seq % LRU_SEQ_TILE == 0 and off_meta % N_META == 0
    real = pl.BlockSpec((LRU_SEQ_TILE, cg), lambda g, b, c: (b * nt + c, g))
    meta = pl.BlockSpec((N_META, cg), lambda g, b, c: (off_meta // N_META + b, g))
    return pl.pallas_call(
        _lru_prompt_kernel, grid=(D_RNN // cg, batch, nt),
        in_specs=[real, real, meta, meta] + _lru_param_specs(lambda g, b, c: g),
        out_specs=[real,
                   pl.BlockSpec((N_META, cg), lambda g, b, c: (b, g)),
                   pl.BlockSpec((None, 1, cg), lambda g, b, c: (b, 0, g))],
        out_shape=[jax.ShapeDtypeStruct((rows, D_RNN), MXU_DTYPE),
                   jax.ShapeDtypeStruct((batch * N_META, D_RNN), MXU_DTYPE),
                   jax.ShapeDtypeStruct((batch, 1, D_RNN), F32)],
        scratch_shapes=[pltpu.VMEM((V7X_SUBLANES, cg), F32)]
        + [pltpu.VMEM((V7X_SUBLANES + LRU_SEQ_TILE, cg), F32)] * 2,
        compiler_params=_cparams(3), name="rglru_prompt",
    )(xb, gate, xb, gate, *params)


def _lru_sample_kernel(x_ref, gate_ref, c0_ref, h0_ref, cw_ref, cb_ref, wa_ref, wx_ref, ba_ref, bx_ref,
                       cneg_ref, y_ref, hl_ref):
    ds = x_ref.shape[0]
    cw, cb = cw_ref[...], cb_ref[...]
    xp = [c0_ref[j] for j in range(CONV_W - 1)] + [x_ref[t] for t in range(ds)]
    xcs = []
    for t in range(ds):
        xc = cb + cw[0:1] * xp[t]
        for j in range(1, CONV_W):
            xc = xc + cw[j:j + 1] * xp[t + j]
        xcs.append(xc)
    xc = jnp.concatenate(xcs, axis=0)
    a, bb = _lru_gates(xc, wa_ref[...], wx_ref[...], ba_ref[...], bx_ref[...], cneg_ref[...])
    db = x_ref.shape[1]
    h = h0_ref[...]
    for t in range(ds):
        rs = slice(t * db, (t + 1) * db)
        h = a[rs] * h + bb[rs]
        y_ref[t] = (gate_ref[t] * h).astype(y_ref.dtype)
    hl_ref[...] = h


def _lru_sample(xb, gate, conv0, h0, params):
    ds, db, _ = xb.shape
    cg = LRU_GROUP
    tok = pl.BlockSpec((ds, db, cg), lambda g: (0, 0, g))
    return pl.pallas_call(
        _lru_sample_kernel, grid=(D_RNN // cg,),
        in_specs=[tok, tok,
                  pl.BlockSpec((CONV_W - 1, db, cg), lambda g: (0, 0, g)),
                  pl.BlockSpec((db, cg), lambda g: (0, g))] + _lru_param_specs(lambda g: g),
        out_specs=[tok, pl.BlockSpec((db, cg), lambda g: (0, g))],
        out_shape=[jax.ShapeDtypeStruct((ds, db, D_RNN), MXU_DTYPE),
                   jax.ShapeDtypeStruct((db, D_RNN), F32)],
        compiler_params=_cparams(1), name="rglru_sample",
    )(xb, gate, conv0, h0, *params)


def _pool_project(means, x, w_ref, scale, g, b):
    ys = []
    for gi in range(len(POOL_WINDOWS)):
        cs = slice(gi * POOL_GROUP, (gi + 1) * POOL_GROUP)
        ys.append(_mm(means[gi] - x[:, cs], w_ref[gi]))
    y = jnp.concatenate(ys, axis=1) * scale
    return _layer_norm_rows(ALPHA * x + y, g, b)


def _pool_tile(x, carry, pos0, w_ref, scale, g, b):
    n = x.shape[0]
    pg = POOL_GROUP

    def shift(v, prev8, k):
        if k < V7X_SUBLANES:
            return _shift_rows(v, prev8, k)
        if n == V7X_SUBLANES:
            return prev8
        return jnp.concatenate([prev8, v[:n - V7X_SUBLANES]], axis=0)

    levels = [x]
    for li, k in enumerate((1, 2, 4, 8)):
        v = levels[-1][:, pg:] if li > 0 else levels[-1]
        levels.append(v + shift(v, carry[li][:, D_MODEL - v.shape[1]:], k))
    pos = (pos0 + lax.broadcasted_iota(jnp.int32, (n, 1), 0)).astype(F32)
    means = []
    for gi, w in enumerate(POOL_WINDOWS):
        cnt = jnp.minimum(float(w), pos + 1.0)
        means.append(levels[gi + 1][:, :pg] / cnt)
    out = _pool_project(means, x, w_ref, scale, g, b)
    new_carry = []
    for li in range(4):
        v = levels[li]
        tail = v[n - V7X_SUBLANES:]
        if v.shape[1] < D_MODEL:
            tail = jnp.concatenate([jnp.zeros((V7X_SUBLANES, D_MODEL - v.shape[1]), F32), tail], axis=1)
        new_carry.append(tail)
    return out, new_carry


def _pool_prompt_kernel(x_ref, xm_ref, w_ref, sc_ref, g_ref, b_ref, o_ref, obf_ref, om_ref, ombf_ref,
                        carry_ref):
    c = pl.program_id(1)
    consts = (w_ref, sc_ref[...], g_ref[...], b_ref[...])

    @pl.when(c == 0)
    def _():
        zero = jnp.zeros((V7X_SUBLANES, D_MODEL), F32)
        out, carry = _pool_tile(xm_ref[...], [zero] * 4, 0, *consts)
        om_ref[...] = out
        ombf_ref[...] = out.astype(ombf_ref.dtype)
        for li in range(4):
            carry_ref[li] = carry[li]

    out, carry = _pool_tile(x_ref[...], [carry_ref[li] for li in range(4)],
                            N_META + c * x_ref.shape[0], *consts)
    o_ref[...] = out
    obf_ref[...] = out.astype(obf_ref.dtype)
    for li in range(4):
        carry_ref[li] = carry[li]


def _pool_prompt(x, w, scale, g, b, batch, seq, off_meta):
    rows = x.shape[0]
    nt = seq // SEQ_TILE
    ng = len(POOL_WINDOWS)
    assert seq % SEQ_TILE == 0 and off_meta % N_META == 0
    real = pl.BlockSpec((SEQ_TILE, D_MODEL), lambda bb, c: (bb * nt + c, 0))
    vec = pl.BlockSpec((1, D_MODEL), lambda bb, c: (0, 0))
    meta_out = pl.BlockSpec((N_META, D_MODEL), lambda bb, c: (bb, 0))
    return pl.pallas_call(
        _pool_prompt_kernel, grid=(batch, nt),
        in_specs=[real, pl.BlockSpec((N_META, D_MODEL), lambda bb, c: (off_meta // N_META + bb, 0)),
                  pl.BlockSpec((ng, POOL_GROUP, POOL_GROUP), lambda bb, c: (0, 0, 0)), vec, vec, vec],
        out_specs=[real, real, meta_out, meta_out],
        out_shape=[jax.ShapeDtypeStruct((rows, D_MODEL), F32), jax.ShapeDtypeStruct((rows, D_MODEL), MXU_DTYPE),
                   jax.ShapeDtypeStruct((batch * N_META, D_MODEL), F32),
                   jax.ShapeDtypeStruct((batch * N_META, D_MODEL), MXU_DTYPE)],
        scratch_shapes=[pltpu.VMEM((4, V7X_SUBLANES, D_MODEL), F32)],
        compiler_params=_cparams(2), name="pool_prompt",
    )(x, x, w, scale.reshape(1, D_MODEL), g.reshape(1, D_MODEL), b.reshape(1, D_MODEL))


def _pool_sample_kernel(x_ref, buf_ref, w_ref, sc_ref, g_ref, b_ref, o_ref, obf_ref, *, pos0):
    ds = x_ref.shape[0]
    nbuf = buf_ref.shape[0]
    pg = POOL_GROUP
    means = []
    for gi, w in enumerate(POOL_WINDOWS):
        cs = slice(gi * pg, (gi + 1) * pg)
        xp = [buf_ref[j, :, cs] for j in range(nbuf - (w - 1), nbuf)] + [x_ref[t, :, cs] for t in range(ds)]
        rows = []
        for t in range(ds):
            acc = xp[t]
            for j in range(1, w):
                acc = acc + xp[t + j]
            rows.append(acc / float(min(w, pos0 + t + 1)))
        means.append(jnp.concatenate(rows, axis=0))
    x = jnp.concatenate([x_ref[t] for t in range(ds)], axis=0)
    out = _pool_project(means, x, w_ref, sc_ref[...], g_ref[...], b_ref[...])
    bs = x_ref.shape[1]
    for t in range(ds):
        o_ref[t] = out[t * bs:(t + 1) * bs]
        obf_ref[t] = out[t * bs:(t + 1) * bs].astype(obf_ref.dtype)


def _pool_sample(x, buf, w, scale, g, b, pos0):
    ds, db, _ = x.shape
    nbuf = buf.shape[0]
    bs = min(POOL_SEQ_BLOCK, db)
    ng = len(POOL_WINDOWS)
    assert db % bs == 0
    tok = pl.BlockSpec((ds, bs, D_MODEL), lambda s: (0, s, 0))
    vec = pl.BlockSpec((1, D_MODEL), lambda s: (0, 0))
    return pl.pallas_call(
        functools.partial(_pool_sample_kernel, pos0=pos0), grid=(db // bs,),
        in_specs=[tok, pl.BlockSpec((nbuf, bs, D_MODEL), lambda s: (0, s, 0)),
                  pl.BlockSpec((ng, POOL_GROUP, POOL_GROUP), lambda s: (0, 0, 0)), vec, vec, vec],
        out_specs=[tok, tok],
        out_shape=[jax.ShapeDtypeStruct((ds, db, D_MODEL), F32), jax.ShapeDtypeStruct((ds, db, D_MODEL), MXU_DTYPE)],
        compiler_params=_cparams(1), name="pool_sample",
    )(x, buf, w, scale.reshape(1, D_MODEL), g.reshape(1, D_MODEL), b.reshape(1, D_MODEL))


def _block_diag_groups(w):
    per = LRU_GROUP // LRU_BLOCK
    groups = []
    for g in range(LRU_BLOCKS // per):
        rows = [jnp.pad(w[g * per + p], ((0, 0), (p * LRU_BLOCK, (per - 1 - p) * LRU_BLOCK))) for p in range(per)]
        groups.append(jnp.concatenate(rows, axis=0))
    return jnp.stack(groups)


def kernel(x_prompt, x_sample, state_ret, state_lru_h, state_lru_conv, state_pool, meta_tokens,
           ret_w_in, ret_gn_g, ret_w_out, lru_w_in, lru_conv_w, lru_conv_b, lru_w_a, lru_b_a,
           lru_w_x, lru_b_x, lru_lambda, lru_w_out, pool_w, pool_scale, ln_mix_g, ln_mix_b,
           ln_ffn_g, ln_ffn_b, ffn_w1, ffn_w3, ffn_w2):
    batch, seq, d = x_prompt.shape
    db, ds, _ = x_sample.shape
    assert d == D_MODEL
    n_real, n_samp, n_meta = batch * seq, db * ds, batch * N_META
    off_s, off_m = n_real, n_real + n_samp
    mx = MXU_DTYPE

    x = jnp.concatenate([
        x_prompt.reshape(n_real, d),
        jnp.swapaxes(x_sample, 0, 1).reshape(n_samp, d),
        jnp.broadcast_to(meta_tokens[None].astype(x_prompt.dtype), (batch, N_META, d)).reshape(n_meta, d),
    ], axis=0)
    xbf = x.astype(mx)

    t_real = jnp.tile(jnp.arange(seq, dtype=F32), batch)
    t_samp = jnp.repeat(jnp.arange(ds, dtype=F32), db)
    t_meta = jnp.tile(jnp.arange(N_META, dtype=F32), batch)
    pos = jnp.concatenate([N_META + t_real, PAST_LEN + t_samp, t_meta])
    half = RET_DK // 2
    inv = ROPE_BASE ** (-jnp.arange(half, dtype=F32) / half)
    ang = pos[:, None] * inv[None, :]
    cos_t, sin_t = jnp.cos(ang), jnp.sin(ang)
    e_rows = jnp.concatenate([
        (RET_CHUNK - 1.0) - jnp.mod(t_real, float(RET_CHUNK)), (ds - 1.0) - t_samp, (N_META - 1.0) - t_meta,
    ])[:, None]
    lg_cols = jnp.repeat(jnp.asarray(_LOG_G, F32), RET_DK)[None, :]

    def sample_rows(a):
        return a[off_s:off_m].reshape(ds, db, a.shape[1])

    def merge(full, samp, meta):
        full = lax.dynamic_update_slice(full, samp.reshape(n_samp, samp.shape[-1]), (off_s, 0))
        return lax.dynamic_update_slice(full, meta, (off_m, 0))

    pad_rows = -(-ds // V7X_BF16_ROWS) * V7X_BF16_ROWS

    def to_seq_major(a):
        a = jnp.swapaxes(a, 0, 1)
        return jnp.pad(a, ((0, 0), (0, pad_rows - ds), (0, 0)))

    def last_rows(a, n):
        return jnp.stack([a[(b + 1) * seq - n:(b + 1) * seq] for b in range(batch)])

    def stacked_rows(w):
        return w.reshape(1, w.shape[0] * w.shape[1], w.shape[2])

    w2_mx = {}

    n_ret = ret_w_in.shape[0]
    ret_p = ret_s = None
    outs_h_p, outs_h_s, outs_conv_p, outs_conv_s, outs_pool_p, outs_pool_s = [], [], [], [], [], []

    for i in range(DEPTH):
        j, kind = i // 3, i % 3
        if kind == 0:
            first = j == 0
            q, w2_mx[i] = _project(_epi_q, xbf, [ret_w_in], j, [0], D_MODEL, 512, ROW_TILE, [mx],
                                   row_inputs=[cos_t, sin_t], side=(ffn_w2, i), name="ret_q")
            k, kd, *side_k = _project(_epi_k, xbf, [ret_w_in], j, [D_MODEL], D_MODEL, 512, ROW_TILE, [mx, mx],
                                      row_inputs=[cos_t, sin_t, e_rows], col_inputs=[lg_cols],
                                      side=(stacked_rows(lru_w_out), 0) if first else None, name="ret_k")
            v, *side_v = _project(_epi_cast, xbf, [ret_w_in], j, [2 * D_MODEL], D_MODEL, 512, ROW_TILE, [mx],
                                  side=(ffn_w2, i + 2) if first and i + 2 < DEPTH else None, name="ret_v")
            sg, *side_g = _project(_epi_silu, xbf, [ret_w_in], j, [3 * D_MODEL], D_MODEL, 512, ROW_TILE, [F32],
                                   side=(stacked_rows(ret_w_out), 0) if first else None, name="ret_g")
            if first:
                lru_w_out_mx = side_k[0].reshape(lru_w_out.shape)
                ret_w_out_mx = side_g[0].reshape(ret_w_out.shape)
                if side_v:
                    w2_mx[i + 2] = side_v[0]
            z, z_meta, ret_p = _retention_prompt(q, k, kd, v, sg, ret_gn_g[j], batch, seq, off_m, j, n_ret, ret_p)
            z_s, ret_s = _retention_sample(*(to_seq_major(sample_rows(a)) for a in (q, k, kd, v, sg)),
                                           ret_gn_g[j], state_ret, j, ds, ret_s)
            z = merge(z, jnp.swapaxes(z_s[:, :ds], 0, 1), z_meta)
            x, xbf = _residual_ln(z, ret_w_out_mx, j, x, ln_mix_g[i], ln_mix_b[i], *ACC_TILES_WIDE_K, name="ret_out")
        elif kind == 1:
            gate, w2_mx[i] = _project(_epi_gelu, xbf, [lru_w_in], j, [0], D_RNN, 512, ROW_TILE, [F32],
                                      side=(ffn_w2, i), name="lru_gate")
            (xb,) = _project(_epi_cast, xbf, [lru_w_in], j, [D_RNN], D_RNN, 512, ROW_TILE, [F32], name="lru_x")
            params = (lru_conv_w[j], lru_conv_b[j].reshape(1, D_RNN),
                      _block_diag_groups(lru_w_a[j]).astype(mx), _block_diag_groups(lru_w_x[j]).astype(mx),
                      lru_b_a[j].reshape(1, D_RNN), lru_b_x[j].reshape(1, D_RNN),
                      (-LRU_C * jax.nn.softplus(-lru_lambda[j].astype(F32))).reshape(1, D_RNN))
            y, y_meta, h_p = _lru_prompt(xb, gate, params, batch, seq, off_m)
            xb_s = sample_rows(xb)
            y_s, h_s = _lru_sample(xb_s, sample_rows(gate), jnp.swapaxes(state_lru_conv[j], 0, 1).astype(F32),
                                   state_lru_h[j].astype(F32), params)
            y = merge(y, y_s, y_meta)
            x, xbf = _residual_ln(y, lru_w_out_mx, j, x, ln_mix_g[i], ln_mix_b[i], *ACC_TILES_WIDE_K, name="lru_out")
            outs_h_p.append(h_p.reshape(batch, D_RNN))
            outs_h_s.append(h_s)
            assert seq >= CONV_W - 1
            outs_conv_p.append(last_rows(xb, CONV_W - 1))
            xp_s = jnp.concatenate([state_lru_conv[j].astype(F32), jnp.swapaxes(xb_s, 0, 1)], axis=1)
            outs_conv_s.append(xp_s[:, -(CONV_W - 1):])
        else:
            p = POOL_MAX - 1
            w_pool = pool_w[j].astype(mx)
            x_s = sample_rows(x)
            assert seq >= p
            outs_pool_p.append(last_rows(x, p))
            xp_s = jnp.concatenate([state_pool[j].astype(F32), jnp.swapaxes(x_s, 0, 1)], axis=1)
            outs_pool_s.append(xp_s[:, -p:])
            xo, xobf, xm, xmbf = _pool_prompt(x, w_pool, pool_scale[j], ln_mix_g[i], ln_mix_b[i], batch, seq, off_m)
            buf_tm = jnp.swapaxes(state_pool[j].astype(F32), 0, 1)
            xs, xsbf = _pool_sample(x_s, buf_tm, w_pool, pool_scale[j], ln_mix_g[i], ln_mix_b[i], PAST_LEN)
            x, xbf = merge(xo, xs, xm), merge(xobf, xsbf, xmbf)
        (hid,) = _project(_epi_swiglu, xbf, [ffn_w1, ffn_w3], i, [0, 0], D_FF, 512, FFN_ROW_TILE, [mx],
                          name="ffn_up")
        x, xbf = _residual_ln(hid, w2_mx[i][None], 0, x, ln_ffn_g[i], ln_ffn_b[i], *ACC_TILES_WIDE_K,
                              name="ffn_down")

    y_prompt = x[:n_real].reshape(batch, seq, d)
    y_sample = jnp.swapaxes(x[off_s:off_m].reshape(ds, db, d), 0, 1)
    return (y_prompt, y_sample, ret_p, ret_s, jnp.stack(outs_h_p),
            jnp.stack(outs_h_s), jnp.stack(outs_conv_p), jnp.stack(outs_conv_s), jnp.stack(outs_pool_p),
            jnp.stack(outs_pool_s))
```
